```python
import math
import jax
import jax.numpy as jnp
from jax import lax
import numpy as np

D_MODEL = 4096
BATCH = 4
SEQ = 2048
DEPTH = 2
DEC_BATCH = 8
DEC_SEQ = 1
PAST_LEN = 16384
PAGE_SIZE = 128

HEAD_DIM = 128
GM_GROUPS = 8
GM_WIDTH = GM_GROUPS * HEAD_DIM
CHUNK = 128
SB_HEADS = 12
SB_WIDTH = SB_HEADS * HEAD_DIM
DA_HEADS = 6
DA_VDIM = 2 * HEAD_DIM
DA_WIDTH = DA_HEADS * DA_VDIM
D_MIX = GM_WIDTH + SB_WIDTH + DA_WIDTH
N_IN = 2 * GM_WIDTH + 3 * SB_WIDTH + 3 * DA_WIDTH
REL_BUCKETS = 32
REL_MAX_DIST = 128
MEM_LEN = 256
MEM_HEADS = 4
MEM_WIDTH = MEM_HEADS * HEAD_DIM
D_FF = 11008
CONV_W = 3

Q_BLOCK = 128
ATTN_SCALE = HEAD_DIM ** -0.5
EPS = 1e-6
NEG = -1e30

kernel_name = 'hybrid_gmlp_stickbreak_diffattn_decoder_step'


def rmsnorm(x, g):
    xf = x.astype(jnp.float32)
    y = xf * lax.rsqrt(jnp.mean(xf * xf, axis=-1, keepdims=True) + EPS)
    return (y * g.astype(jnp.float32)).astype(x.dtype)


def layernorm(x, g, b):
    xf = x.astype(jnp.float32)
    xc = xf - jnp.mean(xf, axis=-1, keepdims=True)
    y = xc * lax.rsqrt(jnp.mean(xc * xc, axis=-1, keepdims=True) + EPS)
    return (y * g.astype(jnp.float32) + b.astype(jnp.float32)).astype(x.dtype)


def rel_bucket(q_pos, k_pos):
    n = jnp.maximum(q_pos[:, None] - k_pos[None, :], 0)
    max_exact = REL_BUCKETS // 2
    nf = jnp.maximum(n, 1).astype(jnp.float32)
    large = max_exact + (jnp.log(nf / max_exact) / math.log(REL_MAX_DIST / max_exact)
                         * (REL_BUCKETS - max_exact)).astype(jnp.int32)
    large = jnp.minimum(large, REL_BUCKETS - 1)
    return jnp.where(n < max_exact, n, large)


def sweep_queries(fn, q, q_pos):
    bsz, t = q.shape[0], q.shape[1]
    if t > Q_BLOCK and t % Q_BLOCK == 0:
        nb = t // Q_BLOCK
        qb = jnp.moveaxis(q.reshape((bsz, nb, Q_BLOCK) + q.shape[2:]), 1, 0)
        pb = q_pos.reshape(nb, Q_BLOCK)
        ob = lax.map(lambda a: fn(a[0], a[1]), (qb, pb))
        ob = jnp.moveaxis(ob, 0, 1)
        return ob.reshape((bsz, t) + ob.shape[3:])
    return fn(q, q_pos)


def stick_breaking(q, k, v, q_pos, k_pos):
    def block(qb, pb):
        z = jnp.einsum('bqhd,bkhd->bhqk', qb, k).astype(jnp.float32) * ATTN_SCALE
        mask = k_pos[None, :] < pb[:, None]
        log_1mb = jnp.where(mask, jax.nn.log_sigmoid(-z), 0.0)
        between = lax.cumsum(log_1mb, axis=3, reverse=True) - log_1mb
        a = jnp.where(mask, jnp.exp(jax.nn.log_sigmoid(z) + between), 0.0)
        return jnp.einsum('bhqk,bkhd->bqhd', a, v.astype(jnp.float32)).astype(v.dtype)
    return sweep_queries(block, q, q_pos)


def diff_attention(q, k, v, q_pos, k_pos, lam, rel_bias):
    def block(qb, pb):
        s = jnp.einsum('bqhmd,bkhmd->bmhqk', qb, k).astype(jnp.float32) * ATTN_SCALE
        bias = rel_bias[rel_bucket(pb, k_pos)].astype(jnp.float32)
        s = s + jnp.transpose(bias, (2, 0, 1))
        mask = k_pos[None, :] <= pb[:, None]
        p = jax.nn.softmax(jnp.where(mask, s, NEG), axis=-1)
        a = p[:, 0] - lam * p[:, 1]
        return jnp.einsum('bhqk,bkhe->bqhe', a, v.astype(jnp.float32)).astype(v.dtype)
    return sweep_queries(block, q, q_pos)


def mem_attention(q, mk, mv):
    s = jnp.einsum('bqhd,bkhd->bhqk', q, mk).astype(jnp.float32) * ATTN_SCALE
    p = jax.nn.softmax(s, axis=-1)
    return jnp.einsum('bhqk,bkhd->bqhd', p, mv.astype(jnp.float32)).astype(q.dtype)


def chunk_spatial_mix(v, ws, bs):
    bsz, t = v.shape[0], v.shape[1]
    wm = ws * jnp.tril(jnp.ones((CHUNK, CHUNK), ws.dtype))
    if t % CHUNK == 0:
        vc = v.reshape(bsz, t // CHUNK, CHUNK, GM_GROUPS, HEAD_DIM)
        out = jnp.einsum('gij,bnjgc->bnigc', wm, vc) + bs.T[None, None, :, :, None]
        return out.reshape(v.shape)
    return jnp.einsum('gij,bjgc->bigc', wm[:, :t, :t], v) + bs[:, :t].T[None, :, :, None]


def conv_ffn(h, prefix, w_gate, conv_w, conv_b, w_up, w_down):
    t_new = h.shape[1]
    g = h @ w_gate
    gp = jnp.concatenate([prefix.astype(g.dtype), g], axis=1)
    gc = conv_b + sum(conv_w[i] * gp[:, i:i + t_new] for i in range(CONV_W))
    y = (jax.nn.silu(gc) * (h @ w_up)) @ w_down
    return y, gp[:, t_new:]


def gather_pages(pool, layer, page_table):
    g = pool[layer, page_table]
    return g.reshape((g.shape[0], g.shape[1] * g.shape[2]) + g.shape[3:])


def decoder_layer(x, past, mem_k, mem_v, conv_prefix, layer_idx, rel_bias, w):
    bsz, t_new = x.shape[0], x.shape[1]
    past_len = 0 if past is None else past[0].shape[1]
    q_pos = past_len + jnp.arange(t_new, dtype=jnp.int32)
    k_pos = jnp.arange(past_len + t_new, dtype=jnp.int32)

    h = rmsnorm(x, w['norm_mix'])
    proj = h @ w['w_in']
    sizes = [2 * GM_WIDTH, SB_WIDTH, SB_WIDTH, SB_WIDTH, DA_WIDTH, DA_WIDTH]
    cuts = [int(c) for c in np.cumsum(sizes)]
    a_in, sb_q, sb_k, sb_v, da_q, da_k, da_v = jnp.split(proj, cuts, axis=-1)

    uv = jax.nn.gelu(a_in)
    u = uv[..., :GM_WIDTH].reshape(bsz, t_new, GM_GROUPS, HEAD_DIM)
    gv = layernorm(uv[..., GM_WIDTH:].reshape(bsz, t_new, GM_GROUPS, HEAD_DIM), w['gm_ln_g'], w['gm_ln_b'])
    a_out = rmsnorm((u * chunk_spatial_mix(gv, w['gm_ws'], w['gm_bs'])).reshape(bsz, t_new, GM_WIDTH),
                    w['norm_a'])

    sb_q = sb_q.reshape(bsz, t_new, SB_HEADS, HEAD_DIM)
    sb_k = sb_k.reshape(bsz, t_new, SB_HEADS, HEAD_DIM)
    sb_v = sb_v.reshape(bsz, t_new, SB_HEADS, HEAD_DIM)
    da_q = da_q.reshape(bsz, t_new, DA_HEADS, 2, HEAD_DIM)
    da_k = da_k.reshape(bsz, t_new, DA_HEADS, 2, HEAD_DIM)
    da_v = da_v.reshape(bsz, t_new, DA_HEADS, DA_VDIM)
    if past is None:
        sb_k_all, sb_v_all, da_k_all, da_v_all = sb_k, sb_v, da_k, da_v
    else:
        sb_k_all = jnp.concatenate([past[0].astype(sb_k.dtype), sb_k], axis=1)
        sb_v_all = jnp.concatenate([past[1].astype(sb_v.dtype), sb_v], axis=1)
        da_k_all = jnp.concatenate([past[2].astype(da_k.dtype), da_k], axis=1)
        da_v_all = jnp.concatenate([past[3].astype(da_v.dtype), da_v], axis=1)

    b_out = stick_breaking(sb_q, sb_k_all, sb_v_all, q_pos, k_pos)
    b_out = rmsnorm(b_out.reshape(bsz, t_new, SB_WIDTH), w['norm_b'])

    lam_init = 0.8 - 0.6 * math.exp(-0.3 * layer_idx)
    f32 = jnp.float32
    lam = (jnp.exp(jnp.sum(w['da_lq1'].astype(f32) * w['da_lk1'].astype(f32)))
           - jnp.exp(jnp.sum(w['da_lq2'].astype(f32) * w['da_lk2'].astype(f32))) + lam_init)
    c = diff_attention(da_q, da_k_all, da_v_all, q_pos, k_pos, lam, rel_bias)
    c_out = (rmsnorm(c, w['da_subln']) * (1.0 - lam_init)).reshape(bsz, t_new, DA_WIDTH)

    x = x + jnp.concatenate([a_out, b_out, c_out], axis=-1) @ w['w_out']

    hm = rmsnorm(x, w['norm_mem'])
    mq = (hm @ w['w_mq']).reshape(bsz, t_new, MEM_HEADS, HEAD_DIM)
    x = x + mem_attention(mq, mem_k.astype(mq.dtype), mem_v.astype(mq.dtype)).reshape(bsz, t_new, MEM_WIDTH) @ w['w_mo']

    hf = rmsnorm(x, w['norm_ffn'])
    y, conv_state = conv_ffn(hf, conv_prefix, w['w_gate'], w['conv_w'], w['conv_b'], w['w_up'], w['w_down'])
    x = x + y
    return x, (sb_k, sb_v, da_k, da_v, gv, conv_state)


def setup_inputs(seed: int = 0) -> dict:
    key = jax.random.key(seed)
    ks = iter(jax.random.split(key, 64))
    f32 = jnp.float32

    def nrm(shape, scale=1.0):
        return scale * jax.random.normal(next(ks), shape, f32)

    def gain(shape):
        return 1.0 + 0.02 * nrm(shape)

    n_pages = PAST_LEN // PAGE_SIZE
    n_used = DEC_BATCH * n_pages
    n_pool = n_used + n_used // 4
    page_table = jax.random.permutation(next(ks), n_pool)[:n_used].reshape(DEC_BATCH, n_pages).astype(jnp.int32)

    return {
        'x_prompt': nrm((BATCH, SEQ, D_MODEL)),
        'x_sample': nrm((DEC_BATCH, DEC_SEQ, D_MODEL)),
        'cache_sb_k': nrm((DEPTH, n_pool, PAGE_SIZE, SB_HEADS, HEAD_DIM)),
        'cache_sb_v': nrm((DEPTH, n_pool, PAGE_SIZE, SB_HEADS, HEAD_DIM)),
        'cache_da_k': nrm((DEPTH, n_pool, PAGE_SIZE, DA_HEADS, 2, HEAD_DIM)),
        'cache_da_v': nrm((DEPTH, n_pool, PAGE_SIZE, DA_HEADS, DA_VDIM)),
        'cache_mem_k': nrm((DEPTH, DEC_BATCH, MEM_LEN, MEM_HEADS, HEAD_DIM)),
        'cache_mem_v': nrm((DEPTH, DEC_BATCH, MEM_LEN, MEM_HEADS, HEAD_DIM)),
        'state_conv': nrm((DEPTH, DEC_BATCH, CONV_W - 1, D_FF)),
        'page_table': page_table,
        'mem_prompt': nrm((BATCH, MEM_LEN, D_MODEL)),
        'norm_mix': gain((DEPTH, D_MODEL)),
        'w_in': nrm((DEPTH, D_MODEL, N_IN), D_MODEL ** -0.5),
        'gm_ln_g': gain((DEPTH, GM_GROUPS, HEAD_DIM)),
        'gm_ln_b': nrm((DEPTH, GM_GROUPS, HEAD_DIM), 0.02),
        'gm_ws': nrm((DEPTH, GM_GROUPS, CHUNK, CHUNK), CHUNK ** -0.5),
        'gm_bs': gain((DEPTH, GM_GROUPS, CHUNK)),
        'norm_a': gain((DEPTH, GM_WIDTH)),
        'norm_b': gain((DEPTH, SB_WIDTH)),
        'da_lq1': nrm((DEPTH, HEAD_DIM), 0.1),
        'da_lk1': nrm((DEPTH, HEAD_DIM), 0.1),
        'da_lq2': nrm((DEPTH, HEAD_DIM), 0.1),
        'da_lk2': nrm((DEPTH, HEAD_DIM), 0.1),
        'da_subln': gain((DEPTH, DA_VDIM)),
        'rel_bias': nrm((REL_BUCKETS, DA_HEADS), 0.5),
        'w_out': nrm((DEPTH, D_MIX, D_MODEL), D_MIX ** -0.5),
        'norm_mem': gain((DEPTH, D_MODEL)),
        'w_mq': nrm((DEPTH, D_MODEL, MEM_WIDTH), D_MODEL ** -0.5),
        'w_mk': nrm((DEPTH, D_MODEL, MEM_WIDTH), D_MODEL ** -0.5),
        'w_mv': nrm((DEPTH, D_MODEL, MEM_WIDTH), D_MODEL ** -0.5),
        'w_mo': nrm((DEPTH, MEM_WIDTH, D_MODEL), MEM_WIDTH ** -0.5),
        'norm_ffn': gain((DEPTH, D_MODEL)),
        'w_gate': nrm((DEPTH, D_MODEL, D_FF), D_MODEL ** -0.5),
        'conv_w': nrm((DEPTH, CONV_W, D_FF), CONV_W ** -0.5),
        'conv_b': nrm((DEPTH, D_FF), 0.02),
        'w_up': nrm((DEPTH, D_MODEL, D_FF), D_MODEL ** -0.5),
        'w_down': nrm((DEPTH, D_FF, D_MODEL), D_FF ** -0.5),
        'norm_final': gain((D_MODEL,)),
    }


def reference(x_prompt, x_sample, cache_sb_k, cache_sb_v, cache_da_k, cache_da_v, cache_mem_k, cache_mem_v,
              state_conv, page_table, mem_prompt, norm_mix, w_in, gm_ln_g, gm_ln_b, gm_ws, gm_bs, norm_a, norm_b,
              da_lq1, da_lk1, da_lq2, da_lk2, da_subln, rel_bias, w_out, norm_mem, w_mq, w_mk, w_mv, w_mo,
              norm_ffn, w_gate, conv_w, conv_b, w_up, w_down, norm_final):
    xp, xs = x_prompt, x_sample
    bp = x_prompt.shape[0]
    p_new = []
    s_new = []
    for l in range(DEPTH):
        w = {
            'norm_mix': norm_mix[l], 'w_in': w_in[l], 'gm_ln_g': gm_ln_g[l], 'gm_ln_b': gm_ln_b[l],
            'gm_ws': gm_ws[l], 'gm_bs': gm_bs[l], 'norm_a': norm_a[l], 'norm_b': norm_b[l],
            'da_lq1': da_lq1[l], 'da_lk1': da_lk1[l], 'da_lq2': da_lq2[l], 'da_lk2': da_lk2[l],
            'da_subln': da_subln[l], 'w_out': w_out[l], 'norm_mem': norm_mem[l], 'w_mq': w_mq[l],
            'w_mo': w_mo[l], 'norm_ffn': norm_ffn[l], 'w_gate': w_gate[l], 'conv_w': conv_w[l],
            'conv_b': conv_b[l], 'w_up': w_up[l], 'w_down': w_down[l],
        }
        mk_p = (mem_prompt @ w_mk[l]).reshape(bp, MEM_LEN, MEM_HEADS, HEAD_DIM)
        mv_p = (mem_prompt @ w_mv[l]).reshape(bp, MEM_LEN, MEM_HEADS, HEAD_DIM)
        conv0 = jnp.zeros((bp, CONV_W - 1, D_FF), xp.dtype)
        xp, (pk, pv, pdk, pdv, _, pconv) = decoder_layer(xp, None, mk_p, mv_p, conv0, l, rel_bias, w)
        p_new.append((pk, pv, pdk, pdv, mk_p, mv_p, pconv))
        past = (gather_pages(cache_sb_k, l, page_table), gather_pages(cache_sb_v, l, page_table),
                gather_pages(cache_da_k, l, page_table), gather_pages(cache_da_v, l, page_table))
        xs, (sk, sv, sdk, sdv, sgv, sconv) = decoder_layer(xs, past, cache_mem_k[l], cache_mem_v[l],
                                                           state_conv[l], l, rel_bias, w)
        s_new.append((sk, sv, sdk, sdv, sgv, sconv))
    y_prompt = rmsnorm(xp, norm_final)
    y_sample = rmsnorm(xs, norm_final)
    p_sb_k = jnp.stack([e[0] for e in p_new])
    p_sb_v = jnp.stack([e[1] for e in p_new])
    p_da_k = jnp.stack([e[2] for e in p_new])
    p_da_v = jnp.stack([e[3] for e in p_new])
    p_mem_k = jnp.stack([e[4] for e in p_new])
    p_mem_v = jnp.stack([e[5] for e in p_new])
    p_conv = jnp.stack([e[6] for e in p_new])
    s_sb_k = jnp.stack([e[0] for e in s_new])
    s_sb_v = jnp.stack([e[1] for e in s_new])
    s_da_k = jnp.stack([e[2] for e in s_new])
    s_da_v = jnp.stack([e[3] for e in s_new])
    s_gm_v = jnp.stack([e[4] for e in s_new])
    s_conv = jnp.stack([e[5] for e in s_new])
    return (y_prompt, y_sample, p_sb_k, p_sb_v, p_da_k, p_da_v, p_mem_k, p_mem_v, p_conv,
            s_sb_k, s_sb_v, s_da_k, s_da_v, s_gm_v, s_conv)
```

```python
import functools
import math

import jax
import jax.numpy as jnp
from jax import lax
from jax.experimental import pallas as pl
from jax.experimental.pallas import tpu as pltpu

F32 = jnp.float32
BF16 = jnp.bfloat16

HEAD_DIM = 128
GM_GROUPS = 8
GM_WIDTH = GM_GROUPS * HEAD_DIM
CHUNK = 128
SB_HEADS = 12
SB_WIDTH = SB_HEADS * HEAD_DIM
DA_HEADS = 6
DA_VDIM = 2 * HEAD_DIM
DA_WIDTH = DA_HEADS * DA_VDIM
MEM_HEADS = 4
MEM_WIDTH = MEM_HEADS * HEAD_DIM
REL_BUCKETS = 32
REL_MAX_DIST = 128
CONV_W = 3
ATTN_SCALE = HEAD_DIM ** -0.5
EPS = 1e-6
NEG = -1e30

SAMPLE_ROWS = 16
ATTN_ROWS = 16
VMEM_LIMIT_BYTES = 56 * 1024 * 1024
FFN_TILE = 512


def _params(*sem):
    return pltpu.CompilerParams(dimension_semantics=sem, vmem_limit_bytes=VMEM_LIMIT_BYTES)


def _dot(a, b):
    return jnp.dot(a, b, preferred_element_type=F32)


def _dot_nt(a, b):
    return lax.dot_general(a, b, (((1,), (1,)), ((), ())), preferred_element_type=F32)


def _softplus(z):
    return jnp.maximum(z, 0.0) + jnp.log(1.0 + jnp.exp(-jnp.abs(z)))


def _split_bf16(x):
    hi = x.astype(BF16)
    lo = (x - hi.astype(F32)).astype(BF16)
    return hi, lo


def _rmsnorm_kernel(x_ref, g_ref, o_ref):
    x = x_ref[...].astype(F32)
    y = x * lax.rsqrt(jnp.mean(x * x, axis=-1, keepdims=True) + EPS)
    o_ref[...] = (y * g_ref[...]).astype(o_ref.dtype)


def rmsnorm(x, g, out_dtype, tm=256):
    m, d = x.shape
    tm = min(tm, m)
    return pl.pallas_call(
        _rmsnorm_kernel,
        grid=(m // tm,),
        in_specs=[pl.BlockSpec((tm, d), lambda i: (i, 0)),
                  pl.BlockSpec((1, d), lambda i: (0, 0))],
        out_specs=pl.BlockSpec((tm, d), lambda i: (i, 0)),
        out_shape=jax.ShapeDtypeStruct((m, d), out_dtype),
        compiler_params=_params("parallel"),
    )(x, g.reshape(1, d).astype(F32))


def _mm_kernel(*refs, nk, has_res):
    x_ref, w_ref = refs[0], refs[1]
    r_ref = refs[2] if has_res else None
    o_ref = refs[2 + has_res]
    part = _dot(x_ref[...], w_ref[...])
    if nk == 1:
        if has_res:
            part = part + r_ref[...]
        o_ref[...] = part.astype(o_ref.dtype)
        return
    acc_ref = refs[3 + has_res]
    k = pl.program_id(2)

    @pl.when(k == 0)
    def _():
        acc_ref[...] = part

    @pl.when(k > 0)
    def _():
        acc_ref[...] += part

    @pl.when(k == nk - 1)
    def _():
        r = acc_ref[...]
        if has_res:
            r = r + r_ref[...]
        o_ref[...] = r.astype(o_ref.dtype)


def matmul(x, w, layer, *, col0=0, ncols=None, tm=1024, tn=512, nk=1, res=None, out_dtype=F32):
    m, kdim = x.shape
    n_total = w.shape[2]
    ncols = n_total if ncols is None else ncols
    tm = min(tm, m)
    tn = min(tn, ncols)
    while ncols % tn or col0 % tn:
        tn //= 2
    tk = kdim // nk
    jb = col0 // tn
    in_specs = [pl.BlockSpec((tm, tk), lambda i, j, k: (i, k)),
                pl.BlockSpec((None, tk, tn), lambda i, j, k: (layer, k, jb + j))]
    args = [x, w]
    if res is not None:
        in_specs.append(pl.BlockSpec((tm, tn), lambda i, j, k: (i, j)))
        args.append(res)
    scratch = [pltpu.VMEM((tm, tn), F32)] if nk > 1 else []
    return pl.pallas_call(
        functools.partial(_mm_kernel, nk=nk, has_res=res is not None),
        grid=(m // tm, ncols // tn, nk),
        in_specs=in_specs,
        out_specs=pl.BlockSpec((tm, tn), lambda i, j, k: (i, j)),
        out_shape=jax.ShapeDtypeStruct((m, ncols), out_dtype),
        scratch_shapes=scratch,
        compiler_params=_params("parallel", "parallel", "arbitrary"),
    )(*args)


def _gmlp_kernel(a_ref, lng_ref, lnb_ref, ws_ref, bs_ref, na_ref, o_ref, gv_ref, *, single_token):
    uv = jax.nn.gelu(a_ref[...])
    if not single_token:
        r = lax.broadcasted_iota(jnp.int32, (CHUNK, CHUNK), 0)
        c = lax.broadcasted_iota(jnp.int32, (CHUNK, CHUNK), 1)
        tril = r >= c
    outs = []
    for g in range(GM_GROUPS):
        lo, hi = g * HEAD_DIM, (g + 1) * HEAD_DIM
        u = uv[:, lo:hi]
        v = uv[:, GM_WIDTH + lo:GM_WIDTH + hi]
        vc = v - jnp.mean(v, axis=-1, keepdims=True)
        y = vc * lax.rsqrt(jnp.mean(vc * vc, axis=-1, keepdims=True) + EPS)
        gv = y * lng_ref[:, lo:hi] + lnb_ref[:, lo:hi]
        gv_ref[:, lo:hi] = gv
        if single_token:
            mix = ws_ref[:, lo:hi] * gv + bs_ref[:, lo:hi]
        else:
            wm = jnp.where(tril, ws_ref[g], 0.0).astype(BF16)
            mix = _dot(wm, gv.astype(BF16)) + bs_ref[:, g:g + 1]
        outs.append(u * mix)
    t = jnp.concatenate(outs, axis=-1)
    y = t * lax.rsqrt(jnp.mean(t * t, axis=-1, keepdims=True) + EPS)
    o_ref[...] = (y * na_ref[...]).astype(o_ref.dtype)


def gmlp(a_in, ln_g, ln_b, ws, bs, norm_a, *, single_token):
    m = a_in.shape[0]
    rows = m if single_token else CHUNK
    full = lambda shape: pl.BlockSpec(shape, lambda i: (0,) * len(shape))
    if single_token:
        ws_arg = jnp.repeat(ws[:, 0, 0], HEAD_DIM).reshape(1, GM_WIDTH)
        bs_arg = jnp.repeat(bs[:, 0], HEAD_DIM).reshape(1, GM_WIDTH)
    else:
        ws_arg = ws
        bs_arg = bs.T
    return pl.pallas_call(
        functools.partial(_gmlp_kernel, single_token=single_token),
        grid=(m // rows,),
        in_specs=[pl.BlockSpec((rows, 2 * GM_WIDTH), lambda i: (i, 0)),
                  full((1, GM_WIDTH)), full((1, GM_WIDTH)),
                  full(ws_arg.shape), full(bs_arg.shape), full((1, GM_WIDTH))],
        out_specs=[pl.BlockSpec((rows, GM_WIDTH), lambda i: (i, 0)),
                   pl.BlockSpec((rows, GM_WIDTH), lambda i: (i, 0))],
        out_shape=[jax.ShapeDtypeStruct((m, GM_WIDTH), BF16),
                   jax.ShapeDtypeStruct((m, GM_WIDTH), F32)],
        compiler_params=_params("parallel"),
    )(a_in, ln_g.reshape(1, GM_WIDTH), ln_b.reshape(1, GM_WIDTH), ws_arg, bs_arg,
      norm_a.reshape(1, GM_WIDTH))


def _sb_kernel(q_ref, k_ref, v_ref, o_ref, kb_ref, vb_ref, *, t, tq):
    kb_ref[...] = k_ref[...].astype(BF16)
    vb_ref[...] = v_ref[...].astype(BF16)
    r = lax.broadcasted_iota(jnp.int32, (tq, tq), 0)
    c = lax.broadcasted_iota(jnp.int32, (tq, tq), 1)
    later = jnp.where(r > c, 1.0, 0.0).astype(BF16)
    causal = c < r

    def tile(qb, kj, run, acc, masked):
        start = pl.multiple_of(kj * tq, tq)
        z = _dot_nt(qb, kb_ref[pl.ds(start, tq), :]) * ATTN_SCALE
        sp = _softplus(z)
        log_1mb = -sp
        if masked:
            log_1mb = jnp.where(causal, log_1mb, 0.0)
        hi, lo = _split_bf16(log_1mb)
        cum = _dot(hi, later) + _dot(lo, later)
        a = jnp.exp((z - sp) + (run + cum))
        if masked:
            a = jnp.where(causal, a, 0.0)
        acc = acc + _dot(a.astype(BF16), vb_ref[pl.ds(start, tq), :])
        run = run + cum[:, 0:1] + log_1mb[:, 0:1]
        return run, acc

    def q_body(qi, carry):
        qstart = pl.multiple_of(qi * tq, tq)
        qb = q_ref[pl.ds(qstart, tq), :]
        run, acc = tile(qb, qi, jnp.zeros((tq, 1), F32), jnp.zeros((tq, HEAD_DIM), F32), True)

        def k_body(j, rc):
            return tile(qb, qi - 1 - j, rc[0], rc[1], False)

        run, acc = lax.fori_loop(0, qi, k_body, (run, acc))
        o_ref[pl.ds(qstart, tq), :] = acc
        return carry

    lax.fori_loop(0, t // tq, q_body, 0)


def _attn_tq(t):
    return 256 if t % 256 == 0 and t >= 1024 else 128


def sb_attention(q, k, v, bsz, t):
    tq = _attn_tq(t)
    spec = pl.BlockSpec((t, HEAD_DIM), lambda b, h: (b, h))
    return pl.pallas_call(
        functools.partial(_sb_kernel, t=t, tq=tq),
        grid=(bsz, SB_HEADS),
        in_specs=[spec, spec, spec],
        out_specs=spec,
        out_shape=jax.ShapeDtypeStruct((bsz * t, SB_WIDTH), F32),
        scratch_shapes=[pltpu.VMEM((t, HEAD_DIM), BF16), pltpu.VMEM((t, HEAD_DIM), BF16)],
        compiler_params=_params("parallel", "parallel"),
    )(q, k, v)


def rel_bucket_of_distance(n):
    max_exact = REL_BUCKETS // 2
    nf = jnp.maximum(n, 1).astype(F32)
    large = max_exact + (jnp.log(nf / max_exact) / math.log(REL_MAX_DIST / max_exact)
                         * (REL_BUCKETS - max_exact)).astype(jnp.int32)
    large = jnp.minimum(large, REL_BUCKETS - 1)
    return jnp.where(n < max_exact, n, large)


def _bias_from_buckets(buckets, rb_ref, head):
    bias = jnp.zeros(buckets.shape, F32)
    for b in range(REL_BUCKETS):
        bias = jnp.where(buckets == b, rb_ref[b, head], bias)
    return bias


def _da_kernel(lam_ref, rb_ref, q_ref, k_ref, v_ref, bd_ref, bo_ref, g_ref, o_ref,
               kb_ref, vb_ref, biasd_ref, biaso_ref, *, t, tq, out_scale):
    head = pl.program_id(1)
    lam = lam_ref[0]
    kb_ref[...] = k_ref[...].astype(BF16)
    vb_ref[...] = v_ref[...].astype(BF16)
    biasd_ref[...] = _bias_from_buckets(bd_ref[...], rb_ref, head)
    biaso_ref[...] = _bias_from_buckets(bo_ref[...], rb_ref, head)
    bias_far = rb_ref[REL_BUCKETS - 1, head]
    r = lax.broadcasted_iota(jnp.int32, (tq, tq), 0)
    c = lax.broadcasted_iota(jnp.int32, (tq, tq), 1)
    causal = c <= r

    def tile(qb, kj, state, bias, masked):
        start = pl.multiple_of(kj * tq, tq)
        kblk = kb_ref[pl.ds(start, tq), :]
        vblk = vb_ref[pl.ds(start, tq), :]
        new = []
        for mi in range(2):
            m_run, l_run, acc = state[mi]
            s = _dot_nt(qb[:, mi * HEAD_DIM:(mi + 1) * HEAD_DIM],
                        kblk[:, mi * HEAD_DIM:(mi + 1) * HEAD_DIM]) * ATTN_SCALE + bias
            if masked:
                s = jnp.where(causal, s, NEG)
            m_new = jnp.maximum(m_run, jnp.max(s, axis=-1, keepdims=True))
            alpha = jnp.exp(m_run - m_new)
            p = jnp.exp(s - m_new)
            l_new = alpha * l_run + jnp.sum(p, axis=-1, keepdims=True)
            acc = alpha * acc + _dot(p.astype(BF16), vblk)
            new.append((m_new, l_new, acc))
        return tuple(new)

    def q_body(qi, carry):
        qstart = pl.multiple_of(qi * tq, tq)
        qb = q_ref[pl.ds(qstart, tq), :]
        init = tuple((jnp.full((tq, 1), NEG, F32), jnp.zeros((tq, 1), F32), jnp.zeros((tq, DA_VDIM), F32))
                     for _ in range(2))
        state = tile(qb, qi, init, biasd_ref[...], True)

        def k_body(j, st):
            kj = qi - 1 - j
            bias = jnp.where(j == 0, biaso_ref[...], bias_far)
            return tile(qb, kj, st, bias, False)

        state = lax.fori_loop(0, qi, k_body, state)
        (_, l1, a1), (_, l2, a2) = state
        cvec = a1 / l1 - lam * (a2 / l2)
        y = cvec * lax.rsqrt(jnp.mean(cvec * cvec, axis=-1, keepdims=True) + EPS)
        o_ref[pl.ds(qstart, tq), :] = (y * g_ref[...] * out_scale).astype(o_ref.dtype)
        return carry

    lax.fori_loop(0, t // tq, q_body, 0)


def da_attention(q, k, v, lam, rel_bias, subln, bsz, t, out_scale):
    tq = _attn_tq(t)
    pos = jnp.arange(tq, dtype=jnp.int32)
    dist = pos[:, None] - pos[None, :]
    bucket_diag = rel_bucket_of_distance(jnp.maximum(dist, 0))
    bucket_off = rel_bucket_of_distance(dist + tq)
    spec = pl.BlockSpec((t, DA_VDIM), lambda b, h: (b, h))
    smem = pl.BlockSpec(memory_space=pltpu.SMEM)
    const = lambda shape: pl.BlockSpec(shape, lambda b, h: (0,) * len(shape))
    return pl.pallas_call(
        functools.partial(_da_kernel, t=t, tq=tq, out_scale=out_scale),
        grid=(bsz, DA_HEADS),
        in_specs=[smem, smem, spec, spec, spec, const((tq, tq)), const((tq, tq)), const((1, DA_VDIM))],
        out_specs=spec,
        out_shape=jax.ShapeDtypeStruct((bsz * t, DA_WIDTH), BF16),
        scratch_shapes=[pltpu.VMEM((t, DA_VDIM), BF16), pltpu.VMEM((t, DA_VDIM), BF16),
                        pltpu.VMEM((tq, tq), F32), pltpu.VMEM((tq, tq), F32)],
        compiler_params=_params("parallel", "parallel"),
    )(lam.reshape(1), rel_bias, q, k, v, bucket_diag, bucket_off, subln.reshape(1, DA_VDIM))


def _mem_kernel(q_ref, k_ref, v_ref, o_ref):
    outs = []
    for h in range(MEM_HEADS):
        lo, hi = h * HEAD_DIM, (h + 1) * HEAD_DIM
        s = _dot_nt(q_ref[:, lo:hi], k_ref[:, lo:hi].astype(BF16)) * ATTN_SCALE
        p = jnp.exp(s - jnp.max(s, axis=-1, keepdims=True))
        den = jnp.sum(p, axis=-1, keepdims=True)
        outs.append(_dot(p.astype(BF16), v_ref[:, lo:hi].astype(BF16)) / den)
    o_ref[...] = jnp.concatenate(outs, axis=-1).astype(o_ref.dtype)


def mem_attention(q, mk, mv, bsz, t, mem_len, tq=512):
    tq = min(tq, t)
    nq = t // tq
    kv_spec = pl.BlockSpec((mem_len, MEM_WIDTH), lambda b, i: (b, 0))
    return pl.pallas_call(
        _mem_kernel,
        grid=(bsz, nq),
        in_specs=[pl.BlockSpec((tq, MEM_WIDTH), lambda b, i: (b * nq + i, 0)), kv_spec, kv_spec],
        out_specs=pl.BlockSpec((tq, MEM_WIDTH), lambda b, i: (b * nq + i, 0)),
        out_shape=jax.ShapeDtypeStruct((bsz * t, MEM_WIDTH), BF16),
        compiler_params=_params("parallel", "parallel"),
    )(q, mk, mv)


def _silu(x):
    return x / (1.0 + jnp.exp(-x))


def _ffn_kernel(h_ref, wg_ref, wu_ref, cw_ref, cb_ref, act_ref, pc_ref, gbuf_ref, *, tm, tiles_per_seq):
    i = pl.program_id(1)
    first = (i % tiles_per_seq) == 0
    h = h_ref[...]
    g = _dot(h, wg_ref[...])
    u = _dot(h, wu_ref[...])

    @pl.when(first)
    def _():
        gbuf_ref[0:8, :] = jnp.zeros((8, g.shape[1]), F32)

    @pl.when(jnp.logical_not(first))
    def _():
        gbuf_ref[0:8, :] = gbuf_ref[tm:tm + 8, :]

    gbuf_ref[8:tm + 8, :] = g
    g1 = gbuf_ref[7:tm + 7, :]
    g2 = gbuf_ref[6:tm + 6, :]
    gc = cb_ref[...] + cw_ref[0:1, :] * g2 + cw_ref[1:2, :] * g1 + cw_ref[2:3, :] * g
    act_ref[...] = (_silu(gc) * u).astype(act_ref.dtype)

    @pl.when((i % tiles_per_seq) == tiles_per_seq - 1)
    def _():
        pc_ref[...] = g[tm - (CONV_W - 1):tm, :]


def ffn_gate_up(h, wg, wu, conv_w, conv_b, layer, bsz, t, *, tm=1024, tn=512):
    m, d = h.shape
    d_ff = wg.shape[2]
    tm = min(tm, t)
    while d_ff % tn:
        tn //= 2
    tiles_per_seq = t // tm
    wspec = pl.BlockSpec((None, d, tn), lambda j, i: (layer, 0, j))
    return pl.pallas_call(
        functools.partial(_ffn_kernel, tm=tm, tiles_per_seq=tiles_per_seq),
        grid=(d_ff // tn, m // tm),
        in_specs=[pl.BlockSpec((tm, d), lambda j, i: (i, 0)), wspec, wspec,
                  pl.BlockSpec((None, CONV_W, tn), lambda j, i: (layer, 0, j)),
                  pl.BlockSpec((None, 1, tn), lambda j, i: (layer, 0, j))],
        out_specs=[pl.BlockSpec((tm, tn), lambda j, i: (i, j)),
                   pl.BlockSpec((None, CONV_W - 1, tn), lambda j, i: (i // tiles_per_seq, 0, j))],
        out_shape=[jax.ShapeDtypeStruct((m, d_ff), BF16),
                   jax.ShapeDtypeStruct((bsz, CONV_W - 1, d_ff), F32)],
        scratch_shapes=[pltpu.VMEM((tm + 8, tn), F32)],
        compiler_params=_params("parallel", "arbitrary"),
    )(h, wg, wu, conv_w, conv_b.reshape(conv_b.shape[0], 1, d_ff))


def _ffn_step_kernel(h_ref, wg_ref, wu_ref, cw_ref, cb_ref, p0_ref, p1_ref, act_ref, g_ref):
    h = h_ref[...]
    g = _dot(h, wg_ref[...])
    u = _dot(h, wu_ref[...])
    gc = cb_ref[...] + cw_ref[0:1, :] * p0_ref[...] + cw_ref[1:2, :] * p1_ref[...] + cw_ref[2:3, :] * g
    act_ref[...] = (_silu(gc) * u).astype(act_ref.dtype)
    g_ref[...] = g


def ffn_gate_up_step(h, wg, wu, conv_w, conv_b, prefix0, prefix1, layer, *, tn=512):
    m, d = h.shape
    d_ff = wg.shape[2]
    while d_ff % tn:
        tn //= 2
    wspec = pl.BlockSpec((None, d, tn), lambda j: (layer, 0, j))
    row = pl.BlockSpec((m, tn), lambda j: (0, j))
    return pl.pallas_call(
        _ffn_step_kernel,
        grid=(d_ff // tn,),
        in_specs=[pl.BlockSpec((m, d), lambda j: (0, 0)), wspec, wspec,
                  pl.BlockSpec((None, CONV_W, tn), lambda j: (layer, 0, j)),
                  pl.BlockSpec((None, 1, tn), lambda j: (layer, 0, j)), row, row],
        out_specs=[row, row],
        out_shape=[jax.ShapeDtypeStruct((m, d_ff), BF16), jax.ShapeDtypeStruct((m, d_ff), F32)],
        compiler_params=_params("parallel"),
    )(h, wg, wu, conv_w, conv_b.reshape(conv_b.shape[0], 1, d_ff), prefix0, prefix1)


PAGES_PER_STEP = 4
DA_MAPS = 2 * DA_HEADS


def _page_specs(layer, n_pages, block):
    specs = []
    for s in range(PAGES_PER_STEP):
        def imap(b, p, pt, s=s):
            return (layer, pt[b, n_pages - 1 - (p * PAGES_PER_STEP + s)]) + (0,) * len(block)
        specs.append(pl.BlockSpec((None, None) + block, imap))
    return specs


def _sb_step_kernel(pt_ref, q_ref, *refs, n_steps):
    del pt_ref
    k_refs = refs[:PAGES_PER_STEP]
    v_refs = refs[PAGES_PER_STEP:2 * PAGES_PER_STEP]
    g_ref = refs[2 * PAGES_PER_STEP]
    o_ref = refs[2 * PAGES_PER_STEP + 1]
    run_ref, acc_ref = refs[2 * PAGES_PER_STEP + 2:]
    p = pl.program_id(1)
    page = k_refs[0].shape[1]
    rowid = lax.broadcasted_iota(jnp.int32, (ATTN_ROWS, HEAD_DIM), 0)
    q = q_ref[...]
    q_rows = [jnp.where(rowid == h, q, 0.0).astype(BF16) for h in range(SB_HEADS)]

    @pl.when(p == 0)
    def _():
        run_ref[...] = jnp.zeros(run_ref.shape, F32)
        acc_ref[...] = jnp.zeros(acc_ref.shape, F32)

    r = lax.broadcasted_iota(jnp.int32, (page, page), 0)
    c = lax.broadcasted_iota(jnp.int32, (page, page), 1)
    later = jnp.where(r > c, 1.0, 0.0).astype(BF16)
    run = run_ref[:, 0:1]
    acc = acc_ref[...]
    for s in range(PAGES_PER_STEP):
        z = _dot_nt(q_rows[0], k_refs[s][0].astype(BF16))
        for h in range(1, SB_HEADS):
            z = z + _dot_nt(q_rows[h], k_refs[s][h].astype(BF16))
        z = z * ATTN_SCALE
        sp = _softplus(z)
        log_1mb = -sp
        hi, lo = _split_bf16(log_1mb)
        cum = _dot(hi, later) + _dot(lo, later)
        a = jnp.exp((z - sp) + (run + cum)).astype(BF16)
        for h in range(SB_HEADS):
            acc = acc + jnp.where(rowid == h, _dot(a, v_refs[s][h].astype(BF16)), 0.0)
        run = run + jnp.sum(log_1mb, axis=-1, keepdims=True)
    run_ref[...] = jnp.broadcast_to(run, run_ref.shape)
    acc_ref[...] = acc

    @pl.when(p == n_steps - 1)
    def _():
        ms = jnp.sum(jnp.sum(acc * acc, axis=-1, keepdims=True), axis=0, keepdims=True) / SB_WIDTH
        o_ref[...] = acc * lax.rsqrt(ms + EPS) * g_ref[...]


def sb_step_attention(q, cache_k, cache_v, page_table, norm_b, layer):
    nb, n_pages = page_table.shape
    page = cache_k.shape[3]
    n_steps = n_pages // PAGES_PER_STEP
    pad_heads = ((0, 0), (0, ATTN_ROWS - SB_HEADS), (0, 0))
    q_rows = jnp.pad(q.reshape(nb, SB_HEADS, HEAD_DIM), pad_heads)
    gain = jnp.pad(norm_b.reshape(SB_HEADS, HEAD_DIM), pad_heads[1:])
    row = pl.BlockSpec((None, ATTN_ROWS, HEAD_DIM), lambda b, p, pt: (b, 0, 0))
    out = pl.pallas_call(
        functools.partial(_sb_step_kernel, n_steps=n_steps),
        grid_spec=pltpu.PrefetchScalarGridSpec(
            num_scalar_prefetch=1,
            grid=(nb, n_steps),
            in_specs=[row] + _page_specs(layer, n_pages, (SB_HEADS, page, HEAD_DIM)) * 2
                     + [pl.BlockSpec((ATTN_ROWS, HEAD_DIM), lambda b, p, pt: (0, 0))],
            out_specs=row,
            scratch_shapes=[pltpu.VMEM((ATTN_ROWS, HEAD_DIM), F32), pltpu.VMEM((ATTN_ROWS, HEAD_DIM), F32)],
        ),
        out_shape=jax.ShapeDtypeStruct((nb, ATTN_ROWS, HEAD_DIM), F32),
        compiler_params=_params("parallel", "arbitrary"),
    )(page_table, q_rows, *([cache_k] * PAGES_PER_STEP), *([cache_v] * PAGES_PER_STEP), gain)
    return out[:, :SB_HEADS].reshape(nb, SB_WIDTH)


def _da_step_kernel(pt_ref, lam_ref, q_ref, kn_ref, vn_ref, rbt_ref, bl_ref, *refs, n_steps, out_scale):
    del pt_ref
    k_refs = refs[:PAGES_PER_STEP]
    v_refs = refs[PAGES_PER_STEP:2 * PAGES_PER_STEP]
    g_ref = refs[2 * PAGES_PER_STEP]
    o_ref = refs[2 * PAGES_PER_STEP + 1]
    m_ref, l_ref, acc_ref, blast_ref = refs[2 * PAGES_PER_STEP + 2:]
    p = pl.program_id(1)
    half = ATTN_ROWS // 2
    page = v_refs[0].shape[1]
    rowid = lax.broadcasted_iota(jnp.int32, (ATTN_ROWS, HEAD_DIM), 0)
    vrow = lax.broadcasted_iota(jnp.int32, (ATTN_ROWS, DA_VDIM), 0) % half
    q = q_ref[...]
    q_rows = [jnp.where(rowid == half * (j % 2) + j // 2, q, 0.0).astype(BF16) for j in range(DA_MAPS)]
    bias_far = rbt_ref[:, REL_BUCKETS - 1:REL_BUCKETS]

    @pl.when(p == 0)
    def _():
        prod = q.astype(BF16).astype(F32) * kn_ref[...].astype(BF16).astype(F32)
        s_self = jnp.sum(prod, axis=-1, keepdims=True) * ATTN_SCALE + rbt_ref[:, 0:1]
        m_ref[...] = jnp.broadcast_to(s_self, m_ref.shape)
        l_ref[...] = jnp.ones(l_ref.shape, F32)
        acc_ref[...] = vn_ref[...].astype(BF16).astype(F32)
        bias = jnp.zeros(blast_ref.shape, F32)
        for b in range(REL_BUCKETS):
            bias = jnp.where(bl_ref[...] == b, rbt_ref[:, b:b + 1], bias)
        blast_ref[...] = bias

    m_run = m_ref[:, 0:1]
    l_run = l_ref[:, 0:1]
    acc = acc_ref[...]
    for s in range(PAGES_PER_STEP):
        bias = bias_far
        if s == 0:
            bias = jnp.where(p == 0, blast_ref[...], bias_far)
        sc = None
        for j in range(DA_MAPS):
            kj = k_refs[s][pl.ds(j, page, stride=DA_MAPS), :].astype(BF16)
            part = _dot_nt(q_rows[j], kj)
            sc = part if sc is None else sc + part
        sc = sc * ATTN_SCALE + bias
        m_new = jnp.maximum(m_run, jnp.max(sc, axis=-1, keepdims=True))
        alpha = jnp.exp(m_run - m_new)
        pr = jnp.exp(sc - m_new)
        l_run = alpha * l_run + jnp.sum(pr, axis=-1, keepdims=True)
        pb = pr.astype(BF16)
        acc = alpha * acc
        for h in range(DA_HEADS):
            acc = acc + jnp.where(vrow == h, _dot(pb, v_refs[s][h].astype(BF16)), 0.0)
        m_run = m_new
    m_ref[...] = jnp.broadcast_to(m_run, m_ref.shape)
    l_ref[...] = jnp.broadcast_to(l_run, l_ref.shape)
    acc_ref[...] = acc

    @pl.when(p == n_steps - 1)
    def _():
        out = acc / l_run
        cvec = out[0:half, :] - lam_ref[0] * out[half:ATTN_ROWS, :]
        y = cvec * lax.rsqrt(jnp.mean(cvec * cvec, axis=-1, keepdims=True) + EPS)
        o_ref[...] = y * g_ref[...] * out_scale


def da_step_attention(q, k_new, v_new, cache_k, cache_v, page_table, lam, rel_bias, subln, layer, out_scale):
    nb, n_pages = page_table.shape
    page = cache_v.shape[3]
    n_steps = n_pages // PAGES_PER_STEP
    half = ATTN_ROWS // 2

    def map_rows(a):
        a = jnp.swapaxes(a.reshape(nb, DA_HEADS, 2, HEAD_DIM), 1, 2)
        a = jnp.pad(a, ((0, 0), (0, 0), (0, half - DA_HEADS), (0, 0)))
        return a.reshape(nb, ATTN_ROWS, HEAD_DIM)

    v_rows = jnp.pad(v_new.reshape(nb, DA_HEADS, DA_VDIM), ((0, 0), (0, half - DA_HEADS), (0, 0)))
    v_rows = jnp.concatenate([v_rows, v_rows], axis=1)
    rbt = jnp.zeros((ATTN_ROWS, REL_BUCKETS), F32)
    rbt = rbt.at[0:DA_HEADS].set(rel_bias.T).at[half:half + DA_HEADS].set(rel_bias.T)
    bucket_last = rel_bucket_of_distance(page - jnp.arange(page, dtype=jnp.int32)).reshape(1, page)
    row = pl.BlockSpec((None, ATTN_ROWS, HEAD_DIM), lambda b, p, pt: (b, 0, 0))
    const = lambda shape: pl.BlockSpec(shape, lambda b, p, pt: (0,) * len(shape))
    out = pl.pallas_call(
        functools.partial(_da_step_kernel, n_steps=n_steps, out_scale=out_scale),
        grid_spec=pltpu.PrefetchScalarGridSpec(
            num_scalar_prefetch=1,
            grid=(nb, n_steps),
            in_specs=[pl.BlockSpec(memory_space=pltpu.SMEM), row, row,
                      pl.BlockSpec((None, ATTN_ROWS, DA_VDIM), lambda b, p, pt: (b, 0, 0)),
                      const((ATTN_ROWS, REL_BUCKETS)), const((1, page))]
                     + _page_specs(layer, n_pages, (page * DA_MAPS, HEAD_DIM))
                     + _page_specs(layer, n_pages, (DA_HEADS, page, DA_VDIM)) + [const((1, DA_VDIM))],
            out_specs=pl.BlockSpec((None, half, DA_VDIM), lambda b, p, pt: (b, 0, 0)),
            scratch_shapes=[pltpu.VMEM((ATTN_ROWS, HEAD_DIM), F32), pltpu.VMEM((ATTN_ROWS, HEAD_DIM), F32),
                            pltpu.VMEM((ATTN_ROWS, DA_VDIM), F32), pltpu.VMEM((ATTN_ROWS, page), F32)],
        ),
        out_shape=jax.ShapeDtypeStruct((nb, half, DA_VDIM), F32),
        compiler_params=_params("parallel", "arbitrary"),
    )(page_table, lam.reshape(1), map_rows(q), map_rows(k_new), v_rows, rbt, bucket_last,
      *([cache_k] * PAGES_PER_STEP), *([cache_v] * PAGES_PER_STEP), subln.reshape(1, DA_VDIM))
    return out[:, :DA_HEADS].reshape(nb, DA_WIDTH)


W_IN_SEGMENTS = (2 * GM_WIDTH, SB_WIDTH, SB_WIDTH, SB_WIDTH, DA_WIDTH, DA_WIDTH, DA_WIDTH)
W_IN_BF16_OUT = (False, True, False, False, True, False, False)


def _lambda(wts, layer):
    lam_init = 0.8 - 0.6 * math.exp(-0.3 * layer)
    lam = (jnp.exp(jnp.sum(wts['da_lq1'][layer] * wts['da_lk1'][layer]))
           - jnp.exp(jnp.sum(wts['da_lq2'][layer] * wts['da_lk2'][layer])) + lam_init)
    return lam.astype(F32), lam_init


def _prompt_layer(x, mem_prompt_bf16, wts, layer, bsz, t, ff_pad):
    lam, lam_init = _lambda(wts, layer)
    h = rmsnorm(x, wts['norm_mix'][layer], BF16)
    segs = []
    col0 = 0
    for width, as_bf16 in zip(W_IN_SEGMENTS, W_IN_BF16_OUT):
        segs.append(matmul(h, wts['w_in'], layer, col0=col0, ncols=width,
                           out_dtype=BF16 if as_bf16 else F32))
        col0 += width
    a_in, sb_q, sb_k, sb_v, da_q, da_k, da_v = segs

    a_out, _ = gmlp(a_in, wts['gm_ln_g'][layer], wts['gm_ln_b'][layer], wts['gm_ws'][layer],
                    wts['gm_bs'][layer], wts['norm_a'][layer], single_token=False)
    b_out = rmsnorm(sb_attention(sb_q, sb_k, sb_v, bsz, t), wts['norm_b'][layer], BF16)
    c_out = da_attention(da_q, da_k, da_v, lam, wts['rel_bias'], wts['da_subln'][layer], bsz, t,
                         1.0 - lam_init)
    mix = jnp.concatenate([a_out, b_out, c_out], axis=-1)
    x = matmul(mix, wts['w_out'], layer, tn=1024, res=x)

    mem_len = mem_prompt_bf16.shape[0] // bsz
    mk = matmul(mem_prompt_bf16, wts['w_mk'], layer)
    mv = matmul(mem_prompt_bf16, wts['w_mv'], layer)
    hm = rmsnorm(x, wts['norm_mem'][layer], BF16)
    mq = matmul(hm, wts['w_mq'], layer, out_dtype=BF16)
    mo = mem_attention(mq, mk, mv, bsz, t, mem_len)
    x = matmul(mo, wts['w_mo'], layer, tn=1024, res=x)

    hf = rmsnorm(x, wts['norm_ffn'][layer], BF16)
    act, pconv = ffn_gate_up(hf, wts['w_gate'], wts['w_up'], wts['conv_w'], wts['conv_b'], layer, bsz, t)
    x = matmul(act, wts['w_down'], layer, nk=2, res=x)
    return x, (sb_k, sb_v, da_k, da_v, mk, mv, pconv[:, :, :pconv.shape[2] - ff_pad])


def _pad_rows(a, rows, cols=None):
    cols = a.shape[1] if cols is None else cols
    return jnp.pad(a, ((0, rows - a.shape[0]), (0, cols - a.shape[1])))


def _sample_layer(x, caches, mem_k, mem_v, conv_state, page_table, wts, layer, nb):
    lam, lam_init = _lambda(wts, layer)
    cache_sb_k, cache_sb_v, cache_da_k, cache_da_v = caches
    h = rmsnorm(x, wts['norm_mix'][layer], BF16)
    proj = matmul(h, wts['w_in'], layer, tn=1024)
    cuts = [0]
    for width in W_IN_SEGMENTS:
        cuts.append(cuts[-1] + width)
    a_in, sb_q, sb_k, sb_v, da_q, da_k, da_v = (proj[:, cuts[i]:cuts[i + 1]] for i in range(7))

    a_out, gv = gmlp(a_in, wts['gm_ln_g'][layer], wts['gm_ln_b'][layer], wts['gm_ws'][layer],
                     wts['gm_bs'][layer], wts['norm_a'][layer], single_token=True)
    b_out = sb_step_attention(sb_q[:nb], cache_sb_k, cache_sb_v, page_table, wts['norm_b'][layer], layer)
    c_out = da_step_attention(da_q[:nb], da_k[:nb], da_v[:nb], cache_da_k, cache_da_v, page_table, lam,
                              wts['rel_bias'], wts['da_subln'][layer], layer, 1.0 - lam_init)
    mix = jnp.concatenate([a_out, _pad_rows(b_out, SAMPLE_ROWS).astype(BF16),
                           _pad_rows(c_out, SAMPLE_ROWS).astype(BF16)], axis=-1)
    x = matmul(mix, wts['w_out'], layer, tn=1024, res=x)

    hm = rmsnorm(x, wts['norm_mem'][layer], BF16)
    mq = matmul(hm, wts['w_mq'], layer, out_dtype=BF16)
    mem_len = mem_k.shape[2]
    mq_rep = jnp.repeat(mq[:nb], SAMPLE_ROWS, axis=0)
    mo = mem_attention(mq_rep, mem_k[layer].reshape(nb * mem_len, MEM_WIDTH),
                       mem_v[layer].reshape(nb * mem_len, MEM_WIDTH), nb, SAMPLE_ROWS, mem_len)
    mo = _pad_rows(mo.reshape(nb, SAMPLE_ROWS, MEM_WIDTH)[:, 0], SAMPLE_ROWS)
    x = matmul(mo, wts['w_mo'], layer, tn=1024, res=x)

    hf = rmsnorm(x, wts['norm_ffn'][layer], BF16)
    prefix = conv_state[layer]
    d_ff, d_ffp = prefix.shape[2], wts['w_gate'].shape[2]
    act, g = ffn_gate_up_step(hf, wts['w_gate'], wts['w_up'], wts['conv_w'], wts['conv_b'],
                              _pad_rows(prefix[:, 0], SAMPLE_ROWS, d_ffp),
                              _pad_rows(prefix[:, 1], SAMPLE_ROWS, d_ffp), layer)
    x = matmul(act, wts['w_down'], layer, tn=1024, nk=2, res=x)
    sconv = jnp.stack([prefix[:, 1], g[:nb, :d_ff]], axis=1)
    return x, (sb_k[:nb], sb_v[:nb], da_k[:nb], da_v[:nb], gv[:nb], sconv)


def kernel(x_prompt, x_sample, cache_sb_k, cache_sb_v, cache_da_k, cache_da_v, cache_mem_k, cache_mem_v,
           state_conv, page_table, mem_prompt, norm_mix, w_in, gm_ln_g, gm_ln_b, gm_ws, gm_bs, norm_a, norm_b,
           da_lq1, da_lk1, da_lq2, da_lk2, da_subln, rel_bias, w_out, norm_mem, w_mq, w_mk, w_mv, w_mo,
           norm_ffn, w_gate, conv_w, conv_b, w_up, w_down, norm_final):
    bsz, t, d = x_prompt.shape
    nb = x_sample.shape[0]
    depth = w_in.shape[0]
    n_pool, page = cache_sb_k.shape[1], cache_sb_k.shape[2]
    mem_len = mem_prompt.shape[1]
    d_ff = w_gate.shape[2]
    ff_pad = -d_ff % FFN_TILE
    pad_last = lambda a: jnp.pad(a, ((0, 0),) * (a.ndim - 1) + ((0, ff_pad),))
    wts = {
        'norm_mix': norm_mix, 'w_in': w_in.astype(BF16), 'gm_ln_g': gm_ln_g, 'gm_ln_b': gm_ln_b, 'gm_ws': gm_ws,
        'gm_bs': gm_bs, 'norm_a': norm_a, 'norm_b': norm_b, 'da_lq1': da_lq1, 'da_lk1': da_lk1,
        'da_lq2': da_lq2, 'da_lk2': da_lk2, 'da_subln': da_subln, 'rel_bias': rel_bias,
        'w_out': w_out.astype(BF16), 'norm_mem': norm_mem, 'w_mq': w_mq.astype(BF16),
        'w_mk': w_mk.astype(BF16), 'w_mv': w_mv.astype(BF16), 'w_mo': w_mo.astype(BF16),
        'norm_ffn': norm_ffn, 'w_gate': pad_last(w_gate.astype(BF16)), 'conv_w': pad_last(conv_w),
        'conv_b': pad_last(conv_b), 'w_up': pad_last(w_up.astype(BF16)),
        'w_down': jnp.pad(w_down.astype(BF16), ((0, 0), (0, ff_pad), (0, 0))),
    }
    caches = (jnp.swapaxes(cache_sb_k, 2, 3), jnp.swapaxes(cache_sb_v, 2, 3),
              cache_da_k.reshape(depth, n_pool, page * DA_MAPS, HEAD_DIM), jnp.swapaxes(cache_da_v, 2, 3))
    mem_k = cache_mem_k.reshape(depth, nb, cache_mem_k.shape[2], MEM_WIDTH)
    mem_v = cache_mem_v.reshape(depth, nb, cache_mem_v.shape[2], MEM_WIDTH)
    mem_prompt_bf16 = mem_prompt.reshape(bsz * mem_len, d).astype(BF16)

    xp = x_prompt.reshape(bsz * t, d)
    xs = _pad_rows(x_sample.reshape(nb, d), SAMPLE_ROWS)
    p_new, s_new = [], []
    for layer in range(depth):
        xp, p_state = _prompt_layer(xp, mem_prompt_bf16, wts, layer, bsz, t, ff_pad)
        p_new.append(p_state)
        xs, s_state = _sample_layer(xs, caches, mem_k, mem_v, state_conv, page_table, wts, layer, nb)
        s_new.append(s_state)
    g_final = norm_final
    y_prompt = rmsnorm(xp, g_final, F32).reshape(bsz, t, d)
    y_sample = rmsnorm(xs, g_final, F32)[:nb].reshape(nb, 1, d)

    def stack(states, idx, shape):
        return jnp.stack([s[idx] for s in states]).reshape((depth,) + shape)

    return (
        y_prompt, y_sample,
        stack(p_new, 0, (bsz, t, SB_HEADS, HEAD_DIM)), stack(p_new, 1, (bsz, t, SB_HEADS, HEAD_DIM)),
        stack(p_new, 2, (bsz, t, DA_HEADS, 2, HEAD_DIM)), stack(p_new, 3, (bsz, t, DA_HEADS, DA_VDIM)),
        stack(p_new, 4, (bsz, mem_len, MEM_HEADS, HEAD_DIM)), stack(p_new, 5, (bsz, mem_len, MEM_HEADS, HEAD_DIM)),
        stack(p_new, 6, (bsz, CONV_W - 1, d_ff)),
        stack(s_new, 0, (nb, 1, SB_HEADS, HEAD_DIM)), stack(s_new, 1, (nb, 1, SB_HEADS, HEAD_DIM)),
        stack(s_new, 2, (nb, 1, DA_HEADS, 2, HEAD_DIM)), stack(s_new, 3, (nb, 1, DA_HEADS, DA_VDIM)),
        stack(s_new, 4, (nb, 1, GM_GROUPS, HEAD_DIM)), stack(s_new, 5, (nb, CONV_W - 1, d_ff)),
    )
```

```python
import functools
import math

import jax
import jax.numpy as jnp
from jax import lax
from jax.experimental import pallas as pl
from jax.experimental.pallas import tpu as pltpu

F32 = jnp.float32
BF16 = jnp.bfloat16

HEAD_DIM = 128
GM_GROUPS = 8
GM_WIDTH = GM_GROUPS * HEAD_DIM
CHUNK = 128
SB_HEADS = 12
SB_WIDTH = SB_HEADS * HEAD_DIM
DA_HEADS = 6
DA_VDIM = 2 * HEAD_DIM
DA_WIDTH = DA_HEADS * DA_VDIM
MEM_HEADS = 4
MEM_WIDTH = MEM_HEADS * HEAD_DIM
REL_BUCKETS = 32
REL_MAX_DIST = 128
CONV_W = 3
ATTN_SCALE = HEAD_DIM ** -0.5
EPS = 1e-6
NEG = -1e30

SAMPLE_ROWS = 16
ATTN_ROWS = 16
VMEM_LIMIT_BYTES = 56 * 1024 * 1024
FFN_TILE = 512
SB_GROUP = 4
DA_GROUP = 2


def _params(*sem):
    return pltpu.CompilerParams(dimension_semantics=sem, vmem_limit_bytes=VMEM_LIMIT_BYTES)


def _dot(a, b):
    return jnp.dot(a, b, preferred_element_type=F32)


def _dot_nt(a, b):
    return lax.dot_general(a, b, (((1,), (1,)), ((), ())), preferred_element_type=F32)


def _softplus(z):
    return jnp.maximum(z, 0.0) + jnp.log(1.0 + jnp.exp(-jnp.abs(z)))


def _split_bf16(x):
    hi = x.astype(BF16)
    lo = (x - hi.astype(F32)).astype(BF16)
    return hi, lo


def _rmsnorm_kernel(x_ref, g_ref, o_ref):
    x = x_ref[...].astype(F32)
    y = x * lax.rsqrt(jnp.mean(x * x, axis=-1, keepdims=True) + EPS)
    o_ref[...] = (y * g_ref[...]).astype(o_ref.dtype)


def rmsnorm(x, g, out_dtype, tm=256):
    m, d = x.shape
    tm = min(tm, m)
    return pl.pallas_call(
        _rmsnorm_kernel,
        grid=(m // tm,),
        in_specs=[pl.BlockSpec((tm, d), lambda i: (i, 0)),
                  pl.BlockSpec((1, d), lambda i: (0, 0))],
        out_specs=pl.BlockSpec((tm, d), lambda i: (i, 0)),
        out_shape=jax.ShapeDtypeStruct((m, d), out_dtype),
        compiler_params=_params("parallel"),
    )(x, g.reshape(1, d).astype(F32))


def _mm_kernel(*refs, nk, has_res):
    x_ref, w_ref = refs[0], refs[1]
    r_ref = refs[2] if has_res else None
    o_ref = refs[2 + has_res]
    part = _dot(x_ref[...], w_ref[...])
    if nk == 1:
        if has_res:
            part = part + r_ref[...]
        o_ref[...] = part.astype(o_ref.dtype)
        return
    acc_ref = refs[3 + has_res]
    k = pl.program_id(2)

    @pl.when(k == 0)
    def _():
        acc_ref[...] = part

    @pl.when(k > 0)
    def _():
        acc_ref[...] += part

    @pl.when(k == nk - 1)
    def _():
        r = acc_ref[...]
        if has_res:
            r = r + r_ref[...]
        o_ref[...] = r.astype(o_ref.dtype)


def matmul(x, w, layer, *, col0=0, ncols=None, tm=1024, tn=512, nk=1, res=None, out_dtype=F32):
    m, kdim = x.shape
    n_total = w.shape[2]
    ncols = n_total if ncols is None else ncols
    tm = min(tm, m)
    tn = min(tn, ncols)
    while ncols % tn or col0 % tn:
        tn //= 2
    tk = kdim // nk
    jb = col0 // tn
    in_specs = [pl.BlockSpec((tm, tk), lambda i, j, k: (i, k)),
                pl.BlockSpec((None, tk, tn), lambda i, j, k: (layer, k, jb + j))]
    args = [x, w]
    if res is not None:
        in_specs.append(pl.BlockSpec((tm, tn), lambda i, j, k: (i, j)))
        args.append(res)
    scratch = [pltpu.VMEM((tm, tn), F32)] if nk > 1 else []
    return pl.pallas_call(
        functools.partial(_mm_kernel, nk=nk, has_res=res is not None),
        grid=(m // tm, ncols // tn, nk),
        in_specs=in_specs,
        out_specs=pl.BlockSpec((tm, tn), lambda i, j, k: (i, j)),
        out_shape=jax.ShapeDtypeStruct((m, ncols), out_dtype),
        scratch_shapes=scratch,
        compiler_params=_params("parallel", "parallel", "arbitrary"),
    )(*args)


def _mm_heads_kernel(*refs, slab):
    x_ref, w_ref, o_ref = refs[0], refs[1], refs[-1]
    part = _dot(x_ref[...], w_ref[...])
    for s in range(o_ref.shape[0]):
        o_ref[s] = part[:, s * slab:(s + 1) * slab].astype(o_ref.dtype)


def matmul_heads(x, w, layer, *, col0, ncols, slab, bsz, t, depth=1, out_layer=0, stacked=None,
                 tm=1024, tn=512, out_dtype=F32):
    m, kdim = x.shape
    tm = min(tm, t)
    tn = min(tn, ncols)
    jb = col0 // tn
    tiles_per_seq = t // tm
    in_specs = [pl.BlockSpec((tm, kdim), lambda i, j: (i, 0)),
                pl.BlockSpec((None, kdim, tn), lambda i, j: (layer, 0, jb + j))]
    args = [x, w]
    aliases = {}
    if stacked is not None:
        in_specs.append(pl.BlockSpec(memory_space=pl.ANY))
        args.append(stacked)
        aliases = {2: 0}
    return pl.pallas_call(
        functools.partial(_mm_heads_kernel, slab=slab),
        grid=(m // tm, ncols // tn),
        in_specs=in_specs,
        out_specs=pl.BlockSpec((None, None, tn // slab, tm, slab),
                               lambda i, j: (out_layer, i // tiles_per_seq, j, i % tiles_per_seq, 0)),
        out_shape=jax.ShapeDtypeStruct((depth, bsz, ncols // slab, t, slab), out_dtype),
        input_output_aliases=aliases,
        compiler_params=_params("parallel", "parallel"),
    )(*args)


def _gmlp_kernel(a_ref, lng_ref, lnb_ref, ws_ref, bs_ref, na_ref, o_ref, gv_ref, *, single_token):
    uv = jax.nn.gelu(a_ref[...])
    if not single_token:
        r = lax.broadcasted_iota(jnp.int32, (CHUNK, CHUNK), 0)
        c = lax.broadcasted_iota(jnp.int32, (CHUNK, CHUNK), 1)
        tril = r >= c
    outs = []
    for g in range(GM_GROUPS):
        lo, hi = g * HEAD_DIM, (g + 1) * HEAD_DIM
        u = uv[:, lo:hi]
        v = uv[:, GM_WIDTH + lo:GM_WIDTH + hi]
        vc = v - jnp.mean(v, axis=-1, keepdims=True)
        y = vc * lax.rsqrt(jnp.mean(vc * vc, axis=-1, keepdims=True) + EPS)
        gv = y * lng_ref[:, lo:hi] + lnb_ref[:, lo:hi]
        gv_ref[:, lo:hi] = gv
        if single_token:
            mix = ws_ref[:, lo:hi] * gv + bs_ref[:, lo:hi]
        else:
            wm = jnp.where(tril, ws_ref[g], 0.0).astype(BF16)
            mix = _dot(wm, gv.astype(BF16)) + bs_ref[:, g:g + 1]
        outs.append(u * mix)
    t = jnp.concatenate(outs, axis=-1)
    y = t * lax.rsqrt(jnp.mean(t * t, axis=-1, keepdims=True) + EPS)
    o_ref[...] = (y * na_ref[...]).astype(o_ref.dtype)


def gmlp(a_in, ln_g, ln_b, ws, bs, norm_a, *, single_token):
    m = a_in.shape[0]
    rows = m if single_token else CHUNK
    full = lambda shape: pl.BlockSpec(shape, lambda i: (0,) * len(shape))
    if single_token:
        ws_arg = jnp.repeat(ws[:, 0, 0], HEAD_DIM).reshape(1, GM_WIDTH)
        bs_arg = jnp.repeat(bs[:, 0], HEAD_DIM).reshape(1, GM_WIDTH)
    else:
        ws_arg = ws
        bs_arg = bs.T
    return pl.pallas_call(
        functools.partial(_gmlp_kernel, single_token=single_token),
        grid=(m // rows,),
        in_specs=[pl.BlockSpec((rows, 2 * GM_WIDTH), lambda i: (i, 0)),
                  full((1, GM_WIDTH)), full((1, GM_WIDTH)),
                  full(ws_arg.shape), full(bs_arg.shape), full((1, GM_WIDTH))],
        out_specs=[pl.BlockSpec((rows, GM_WIDTH), lambda i: (i, 0)),
                   pl.BlockSpec((rows, GM_WIDTH), lambda i: (i, 0))],
        out_shape=[jax.ShapeDtypeStruct((m, GM_WIDTH), BF16),
                   jax.ShapeDtypeStruct((m, GM_WIDTH), F32)],
        compiler_params=_params("parallel"),
    )(a_in, ln_g.reshape(1, GM_WIDTH), ln_b.reshape(1, GM_WIDTH), ws_arg, bs_arg,
      norm_a.reshape(1, GM_WIDTH))


def _sb_kernel(q_ref, k_ref, v_ref, o_ref, kb_ref, vb_ref, *, t, tq):
    group = q_ref.shape[0]
    heads = range(group)
    kb_ref[...] = k_ref[...].astype(BF16)
    vb_ref[...] = v_ref[...].astype(BF16)
    r = lax.broadcasted_iota(jnp.int32, (tq, tq), 0)
    c = lax.broadcasted_iota(jnp.int32, (tq, tq), 1)
    later = jnp.where(r > c, 1.0, 0.0).astype(BF16)
    causal = c < r

    def tile(qb, kj, state, masked):
        start = pl.multiple_of(kj * tq, tq)
        z = [_dot_nt(qb[g], kb_ref[g, pl.ds(start, tq), :]) * ATTN_SCALE for g in heads]
        sp = [_softplus(z[g]) for g in heads]
        log_1mb = [-sp[g] for g in heads]
        if masked:
            log_1mb = [jnp.where(causal, log_1mb[g], 0.0) for g in heads]
        parts = [_split_bf16(log_1mb[g]) for g in heads]
        cum = [_dot(parts[g][0], later) + _dot(parts[g][1], later) for g in heads]
        a = [jnp.exp((z[g] - sp[g]) + (state[g][0] + cum[g])) for g in heads]
        if masked:
            a = [jnp.where(causal, a[g], 0.0) for g in heads]
        pv = [_dot(a[g].astype(BF16), vb_ref[g, pl.ds(start, tq), :]) for g in heads]
        return tuple((state[g][0] + cum[g][:, 0:1] + log_1mb[g][:, 0:1], state[g][1] + pv[g]) for g in heads)

    def q_body(qi, carry):
        qstart = pl.multiple_of(qi * tq, tq)
        qb = [q_ref[g, pl.ds(qstart, tq), :] for g in heads]
        init = tuple((jnp.zeros((tq, 1), F32), jnp.zeros((tq, HEAD_DIM), F32)) for _ in heads)
        state = tile(qb, qi, init, True)
        state = lax.fori_loop(0, qi, lambda j, st: tile(qb, qi - 1 - j, st, False), state)
        o_ref[pl.ds(qstart, tq), :] = jnp.concatenate([acc for _, acc in state], axis=1)
        return carry

    lax.fori_loop(0, t // tq, q_body, 0)


def _attn_tq(t):
    return 256 if t % 256 == 0 and t >= 1024 else 128


def sb_attention(q, k, v, layer, bsz, t):
    tq = _attn_tq(t)
    group = SB_GROUP

    def spec(lead):
        return pl.BlockSpec((None, None, group, t, HEAD_DIM), lambda b, h: (lead, b, h, 0, 0))

    return pl.pallas_call(
        functools.partial(_sb_kernel, t=t, tq=tq),
        grid=(bsz, SB_HEADS // group),
        in_specs=[spec(0), spec(layer), spec(layer)],
        out_specs=pl.BlockSpec((t, group * HEAD_DIM), lambda b, h: (b, h)),
        out_shape=jax.ShapeDtypeStruct((bsz * t, SB_WIDTH), F32),
        scratch_shapes=[pltpu.VMEM((group, t, HEAD_DIM), BF16), pltpu.VMEM((group, t, HEAD_DIM), BF16)],
        compiler_params=_params("parallel", "parallel"),
    )(q, k, v)


def rel_bucket_of_distance(n):
    max_exact = REL_BUCKETS // 2
    nf = jnp.maximum(n, 1).astype(F32)
    large = max_exact + (jnp.log(nf / max_exact) / math.log(REL_MAX_DIST / max_exact)
                         * (REL_BUCKETS - max_exact)).astype(jnp.int32)
    large = jnp.minimum(large, REL_BUCKETS - 1)
    return jnp.where(n < max_exact, n, large)


def _bias_from_buckets(buckets, rb_ref, head):
    bias = jnp.zeros(buckets.shape, F32)
    for b in range(REL_BUCKETS):
        bias = jnp.where(buckets == b, rb_ref[b, head], bias)
    return bias


def _da_kernel(lam_ref, rb_ref, q_ref, k_ref, v_ref, bd_ref, bo_ref, g_ref, o_ref,
               kb_ref, vb_ref, biasd_ref, biaso_ref, *, t, tq, out_scale):
    group = v_ref.shape[0]
    head0 = pl.program_id(1) * group
    lam = lam_ref[0]
    chains = [(g, mi) for g in range(group) for mi in range(2)]
    cols = [g * DA_VDIM + mi * HEAD_DIM for g, mi in chains]
    kb_ref[...] = k_ref[...].astype(BF16)
    vb_ref[...] = v_ref[...].astype(BF16)
    for g in range(group):
        biasd_ref[g] = _bias_from_buckets(bd_ref[...], rb_ref, head0 + g)
        biaso_ref[g] = _bias_from_buckets(bo_ref[...], rb_ref, head0 + g)
    bias_far = [rb_ref[REL_BUCKETS - 1, head0 + g] for g in range(group)]
    r = lax.broadcasted_iota(jnp.int32, (tq, tq), 0)
    c = lax.broadcasted_iota(jnp.int32, (tq, tq), 1)
    causal = c <= r

    def tile(qb, kj, state, bias, masked):
        start = pl.multiple_of(kj * tq, tq)
        kblk = kb_ref[pl.ds(start, tq), :]
        n = range(len(chains))
        s = [_dot_nt(qb[:, cols[i]:cols[i] + HEAD_DIM], kblk[:, cols[i]:cols[i] + HEAD_DIM]) * ATTN_SCALE
             + bias[chains[i][0]] for i in n]
        if masked:
            s = [jnp.where(causal, s[i], NEG) for i in n]
        m_new = [jnp.maximum(state[i][0], jnp.max(s[i], axis=-1, keepdims=True)) for i in n]
        alpha = [jnp.exp(state[i][0] - m_new[i]) for i in n]
        p = [jnp.exp(s[i] - m_new[i]) for i in n]
        l_new = [alpha[i] * state[i][1] + jnp.sum(p[i], axis=-1, keepdims=True) for i in n]
        pv = [_dot(p[i].astype(BF16), vb_ref[chains[i][0], pl.ds(start, tq), :]) for i in n]
        return tuple((m_new[i], l_new[i], alpha[i] * state[i][2] + pv[i]) for i in n)

    def q_body(qi, carry):
        qstart = pl.multiple_of(qi * tq, tq)
        qb = q_ref[pl.ds(qstart, tq), :]
        init = tuple((jnp.full((tq, 1), NEG, F32), jnp.zeros((tq, 1), F32), jnp.zeros((tq, DA_VDIM), F32))
                     for _ in chains)
        state = tile(qb, qi, init, [biasd_ref[g] for g in range(group)], True)

        def k_body(j, st):
            bias = [jnp.where(j == 0, biaso_ref[g], bias_far[g]) for g in range(group)]
            return tile(qb, qi - 1 - j, st, bias, False)

        state = lax.fori_loop(0, qi, k_body, state)
        outs = []
        for g in range(group):
            (_, l1, a1), (_, l2, a2) = state[2 * g], state[2 * g + 1]
            cvec = a1 / l1 - lam * (a2 / l2)
            y = cvec * lax.rsqrt(jnp.mean(cvec * cvec, axis=-1, keepdims=True) + EPS)
            outs.append(y * g_ref[...] * out_scale)
        o_ref[pl.ds(qstart, tq), :] = jnp.concatenate(outs, axis=1).astype(o_ref.dtype)
        return carry

    lax.fori_loop(0, t // tq, q_body, 0)


def da_attention(q, k, v, layer, lam, rel_bias, subln, bsz, t, out_scale):
    tq = _attn_tq(t)
    group = DA_GROUP
    pos = jnp.arange(tq, dtype=jnp.int32)
    dist = pos[:, None] - pos[None, :]
    bucket_diag = rel_bucket_of_distance(jnp.maximum(dist, 0))
    bucket_off = rel_bucket_of_distance(dist + tq)
    spec = pl.BlockSpec((t, group * DA_VDIM), lambda b, h: (b, h))
    smem = pl.BlockSpec(memory_space=pltpu.SMEM)
    const = lambda shape: pl.BlockSpec(shape, lambda b, h: (0,) * len(shape))
    return pl.pallas_call(
        functools.partial(_da_kernel, t=t, tq=tq, out_scale=out_scale),
        grid=(bsz, DA_HEADS // group),
        in_specs=[smem, smem, spec, spec,
                  pl.BlockSpec((None, None, group, t, DA_VDIM), lambda b, h: (layer, b, h, 0, 0)),
                  const((tq, tq)), const((tq, tq)), const((1, DA_VDIM))],
        out_specs=spec,
        out_shape=jax.ShapeDtypeStruct((bsz * t, DA_WIDTH), BF16),
        scratch_shapes=[pltpu.VMEM((t, group * DA_VDIM), BF16), pltpu.VMEM((group, t, DA_VDIM), BF16),
                        pltpu.VMEM((group, tq, tq), F32), pltpu.VMEM((group, tq, tq), F32)],
        compiler_params=_params("parallel", "parallel"),
    )(lam.reshape(1), rel_bias, q, k, v, bucket_diag, bucket_off, subln.reshape(1, DA_VDIM))


def _mem_kernel(q_ref, k_ref, v_ref, o_ref):
    outs = []
    for h in range(MEM_HEADS):
        lo, hi = h * HEAD_DIM, (h + 1) * HEAD_DIM
        s = _dot_nt(q_ref[:, lo:hi], k_ref[:, lo:hi].astype(BF16)) * ATTN_SCALE
        p = jnp.exp(s - jnp.max(s, axis=-1, keepdims=True))
        den = jnp.sum(p, axis=-1, keepdims=True)
        outs.append(_dot(p.astype(BF16), v_ref[:, lo:hi].astype(BF16)) / den)
    o_ref[...] = jnp.concatenate(outs, axis=-1).astype(o_ref.dtype)


def mem_attention(q, mk, mv, bsz, t, mem_len, tq=512):
    tq = min(tq, t)
    nq = t // tq
    kv_spec = pl.BlockSpec((mem_len, MEM_WIDTH), lambda b, i: (b, 0))
    return pl.pallas_call(
        _mem_kernel,
        grid=(bsz, nq),
        in_specs=[pl.BlockSpec((tq, MEM_WIDTH), lambda b, i: (b * nq + i, 0)), kv_spec, kv_spec],
        out_specs=pl.BlockSpec((tq, MEM_WIDTH), lambda b, i: (b * nq + i, 0)),
        out_shape=jax.ShapeDtypeStruct((bsz * t, MEM_WIDTH), BF16),
        compiler_params=_params("parallel", "parallel"),
    )(q, mk, mv)


def _silu(x):
    return x / (1.0 + jnp.exp(-x))


def _ffn_kernel(h_ref, wg_ref, wu_ref, cw_ref, cb_ref, act_ref, pc_ref, gbuf_ref, *, tm, tiles_per_seq):
    i = pl.program_id(1)
    first = (i % tiles_per_seq) == 0
    h = h_ref[...]
    g = _dot(h, wg_ref[...])
    u = _dot(h, wu_ref[...])

    @pl.when(first)
    def _():
        gbuf_ref[0:8, :] = jnp.zeros((8, g.shape[1]), F32)

    @pl.when(jnp.logical_not(first))
    def _():
        gbuf_ref[0:8, :] = gbuf_ref[tm:tm + 8, :]

    gbuf_ref[8:tm + 8, :] = g
    g1 = gbuf_ref[7:tm + 7, :]
    g2 = gbuf_ref[6:tm + 6, :]
    gc = cb_ref[...] + cw_ref[0:1, :] * g2 + cw_ref[1:2, :] * g1 + cw_ref[2:3, :] * g
    act_ref[...] = (_silu(gc) * u).astype(act_ref.dtype)

    @pl.when((i % tiles_per_seq) == tiles_per_seq - 1)
    def _():
        pc_ref[...] = g[tm - (CONV_W - 1):tm, :]


def ffn_gate_up(h, wg, wu, conv_w, conv_b, layer, bsz, t, *, tm=1024, tn=512):
    m, d = h.shape
    d_ff = wg.shape[2]
    tm = min(tm, t)
    while d_ff % tn:
        tn //= 2
    tiles_per_seq = t // tm
    wspec = pl.BlockSpec((None, d, tn), lambda j, i: (layer, 0, j))
    return pl.pallas_call(
        functools.partial(_ffn_kernel, tm=tm, tiles_per_seq=tiles_per_seq),
        grid=(d_ff // tn, m // tm),
        in_specs=[pl.BlockSpec((tm, d), lambda j, i: (i, 0)), wspec, wspec,
                  pl.BlockSpec((None, CONV_W, tn), lambda j, i: (layer, 0, j)),
                  pl.BlockSpec((None, 1, tn), lambda j, i: (layer, 0, j))],
        out_specs=[pl.BlockSpec((tm, tn), lambda j, i: (i, j)),
                   pl.BlockSpec((None, CONV_W - 1, tn), lambda j, i: (i // tiles_per_seq, 0, j))],
        out_shape=[jax.ShapeDtypeStruct((m, d_ff), BF16),
                   jax.ShapeDtypeStruct((bsz, CONV_W - 1, d_ff), F32)],
        scratch_shapes=[pltpu.VMEM((tm + 8, tn), F32)],
        compiler_params=_params("parallel", "arbitrary"),
    )(h, wg, wu, conv_w, conv_b.reshape(conv_b.shape[0], 1, d_ff))


def _ffn_step_kernel(h_ref, wg_ref, wu_ref, cw_ref, cb_ref, p0_ref, p1_ref, act_ref, g_ref):
    h = h_ref[...]
    g = _dot(h, wg_ref[...])
    u = _dot(h, wu_ref[...])
    gc = cb_ref[...] + cw_ref[0:1, :] * p0_ref[...] + cw_ref[1:2, :] * p1_ref[...] + cw_ref[2:3, :] * g
    act_ref[...] = (_silu(gc) * u).astype(act_ref.dtype)
    g_ref[...] = g


def ffn_gate_up_step(h, wg, wu, conv_w, conv_b, prefix0, prefix1, layer, *, tn=512):
    m, d = h.shape
    d_ff = wg.shape[2]
    while d_ff % tn:
        tn //= 2
    wspec = pl.BlockSpec((None, d, tn), lambda j: (layer, 0, j))
    row = pl.BlockSpec((m, tn), lambda j: (0, j))
    return pl.pallas_call(
        _ffn_step_kernel,
        grid=(d_ff // tn,),
        in_specs=[pl.BlockSpec((m, d), lambda j: (0, 0)), wspec, wspec,
                  pl.BlockSpec((None, CONV_W, tn), lambda j: (layer, 0, j)),
                  pl.BlockSpec((None, 1, tn), lambda j: (layer, 0, j)), row, row],
        out_specs=[row, row],
        out_shape=[jax.ShapeDtypeStruct((m, d_ff), BF16), jax.ShapeDtypeStruct((m, d_ff), F32)],
        compiler_params=_params("parallel"),
    )(h, wg, wu, conv_w, conv_b.reshape(conv_b.shape[0], 1, d_ff), prefix0, prefix1)


PAGES_PER_STEP = 8
DA_MAPS = 2 * DA_HEADS


def _page_specs(layer, n_pages, block):
    specs = []
    for s in range(PAGES_PER_STEP):
        def imap(b, p, pt, s=s):
            return (layer, pt[b, n_pages - 1 - (p * PAGES_PER_STEP + s)]) + (0,) * len(block)
        specs.append(pl.BlockSpec((None, None) + block, imap))
    return specs


def _sb_step_kernel(pt_ref, q_ref, *refs, n_steps):
    del pt_ref
    k_refs = refs[:PAGES_PER_STEP]
    v_refs = refs[PAGES_PER_STEP:2 * PAGES_PER_STEP]
    g_ref = refs[2 * PAGES_PER_STEP]
    o_ref = refs[2 * PAGES_PER_STEP + 1]
    run_ref, acc_ref = refs[2 * PAGES_PER_STEP + 2:]
    p = pl.program_id(1)
    page = k_refs[0].shape[1]
    rowid = lax.broadcasted_iota(jnp.int32, (ATTN_ROWS, HEAD_DIM), 0)
    q = q_ref[...]
    q_rows = [jnp.where(rowid == h, q, 0.0).astype(BF16) for h in range(SB_HEADS)]

    @pl.when(p == 0)
    def _():
        run_ref[...] = jnp.zeros(run_ref.shape, F32)
        acc_ref[...] = jnp.zeros(acc_ref.shape, F32)

    r = lax.broadcasted_iota(jnp.int32, (page, page), 0)
    c = lax.broadcasted_iota(jnp.int32, (page, page), 1)
    later = jnp.where(r > c, 1.0, 0.0).astype(BF16)
    run = run_ref[:, 0:1]
    half = SB_HEADS // 2
    zs = []
    for s in range(PAGES_PER_STEP):
        parts = []
        for h0 in (0, half):
            zp = _dot_nt(q_rows[h0], k_refs[s][h0].astype(BF16))
            for h in range(h0 + 1, h0 + half):
                zp = zp + _dot_nt(q_rows[h], k_refs[s][h].astype(BF16))
            parts.append(zp)
        zs.append((parts[0] + parts[1]) * ATTN_SCALE)
    acc_parts = []
    for s in range(PAGES_PER_STEP):
        z = zs[s]
        sp = _softplus(z)
        log_1mb = -sp
        hi, lo = _split_bf16(log_1mb)
        cum = _dot(hi, later) + _dot(lo, later)
        a = jnp.exp((z - sp) + (run + cum)).astype(BF16)
        part = jnp.zeros((ATTN_ROWS, HEAD_DIM), F32)
        for h in range(SB_HEADS):
            part = part + jnp.where(rowid == h, _dot(a, v_refs[s][h].astype(BF16)), 0.0)
        acc_parts.append(part)
        run = run + jnp.sum(log_1mb, axis=-1, keepdims=True)
    run_ref[...] = jnp.broadcast_to(run, run_ref.shape)
    acc = acc_ref[...] + sum(acc_parts)
    acc_ref[...] = acc

    @pl.when(p == n_steps - 1)
    def _():
        ms = jnp.sum(jnp.sum(acc * acc, axis=-1, keepdims=True), axis=0, keepdims=True) / SB_WIDTH
        o_ref[...] = acc * lax.rsqrt(ms + EPS) * g_ref[...]


def sb_step_attention(q, cache_k, cache_v, page_table, norm_b, layer):
    nb, n_pages = page_table.shape
    page = cache_k.shape[3]
    n_steps = n_pages // PAGES_PER_STEP
    pad_heads = ((0, 0), (0, ATTN_ROWS - SB_HEADS), (0, 0))
    q_rows = jnp.pad(q.reshape(nb, SB_HEADS, HEAD_DIM), pad_heads)
    gain = jnp.pad(norm_b.reshape(SB_HEADS, HEAD_DIM), pad_heads[1:])
    row = pl.BlockSpec((None, ATTN_ROWS, HEAD_DIM), lambda b, p, pt: (b, 0, 0))
    out = pl.pallas_call(
        functools.partial(_sb_step_kernel, n_steps=n_steps),
        grid_spec=pltpu.PrefetchScalarGridSpec(
            num_scalar_prefetch=1,
            grid=(nb, n_steps),
            in_specs=[row] + _page_specs(layer, n_pages, (SB_HEADS, page, HEAD_DIM)) * 2
                     + [pl.BlockSpec((ATTN_ROWS, HEAD_DIM), lambda b, p, pt: (0, 0))],
            out_specs=row,
            scratch_shapes=[pltpu.VMEM((ATTN_ROWS, HEAD_DIM), F32), pltpu.VMEM((ATTN_ROWS, HEAD_DIM), F32)],
        ),
        out_shape=jax.ShapeDtypeStruct((nb, ATTN_ROWS, HEAD_DIM), F32),
        compiler_params=_params("parallel", "arbitrary"),
    )(page_table, q_rows, *([cache_k] * PAGES_PER_STEP), *([cache_v] * PAGES_PER_STEP), gain)
    return out[:, :SB_HEADS].reshape(nb, SB_WIDTH)


def _da_step_kernel(pt_ref, lam_ref, q_ref, kn_ref, vn_ref, rbt_ref, bl_ref, *refs, n_steps, out_scale):
    del pt_ref
    k_refs = refs[:PAGES_PER_STEP]
    v_refs = refs[PAGES_PER_STEP:2 * PAGES_PER_STEP]
    g_ref = refs[2 * PAGES_PER_STEP]
    o_ref = refs[2 * PAGES_PER_STEP + 1]
    m_ref, l_ref, acc_ref, blast_ref = refs[2 * PAGES_PER_STEP + 2:]
    p = pl.program_id(1)
    half = ATTN_ROWS // 2
    page = v_refs[0].shape[1]
    rowid = lax.broadcasted_iota(jnp.int32, (ATTN_ROWS, HEAD_DIM), 0)
    vrow = lax.broadcasted_iota(jnp.int32, (ATTN_ROWS, DA_VDIM), 0) % half
    q = q_ref[...]
    q_rows = [jnp.where(rowid == half * (j % 2) + j // 2, q, 0.0).astype(BF16) for j in range(DA_MAPS)]
    bias_far = rbt_ref[:, REL_BUCKETS - 1:REL_BUCKETS]

    @pl.when(p == 0)
    def _():
        prod = q.astype(BF16).astype(F32) * kn_ref[...].astype(BF16).astype(F32)
        s_self = jnp.sum(prod, axis=-1, keepdims=True) * ATTN_SCALE + rbt_ref[:, 0:1]
        m_ref[...] = jnp.broadcast_to(s_self, m_ref.shape)
        l_ref[...] = jnp.ones(l_ref.shape, F32)
        acc_ref[...] = vn_ref[...].astype(BF16).astype(F32)
        bias = jnp.zeros(blast_ref.shape, F32)
        for b in range(REL_BUCKETS):
            bias = jnp.where(bl_ref[...] == b, rbt_ref[:, b:b + 1], bias)
        blast_ref[...] = bias

    m_run = m_ref[:, 0:1]
    l_run = l_ref[:, 0:1]
    acc = acc_ref[...]
    scs = []
    for s in range(PAGES_PER_STEP):
        bias = bias_far
        if s == 0:
            bias = jnp.where(p == 0, blast_ref[...], bias_far)
        parts = []
        for j0 in (0, DA_HEADS):
            part = None
            for j in range(j0, j0 + DA_HEADS):
                kj = k_refs[s][pl.ds(j, page, stride=DA_MAPS), :].astype(BF16)
                d = _dot_nt(q_rows[j], kj)
                part = d if part is None else part + d
            parts.append(part)
        scs.append((parts[0] + parts[1]) * ATTN_SCALE + bias)
    m_new = m_run
    for sc in scs:
        m_new = jnp.maximum(m_new, jnp.max(sc, axis=-1, keepdims=True))
    alpha = jnp.exp(m_run - m_new)
    l_run = alpha * l_run
    acc_parts = []
    for s in range(PAGES_PER_STEP):
        pr = jnp.exp(scs[s] - m_new)
        l_run = l_run + jnp.sum(pr, axis=-1, keepdims=True)
        pb = pr.astype(BF16)
        part = jnp.zeros((ATTN_ROWS, DA_VDIM), F32)
        for h in range(DA_HEADS):
            part = part + jnp.where(vrow == h, _dot(pb, v_refs[s][h].astype(BF16)), 0.0)
        acc_parts.append(part)
    acc = alpha * acc + sum(acc_parts)
    m_run = m_new
    m_ref[...] = jnp.broadcast_to(m_run, m_ref.shape)
    l_ref[...] = jnp.broadcast_to(l_run, l_ref.shape)
    acc_ref[...] = acc

    @pl.when(p == n_steps - 1)
    def _():
        out = acc / l_run
        cvec = out[0:half, :] - lam_ref[0] * out[half:ATTN_ROWS, :]
        y = cvec * lax.rsqrt(jnp.mean(cvec * cvec, axis=-1, keepdims=True) + EPS)
        o_ref[...] = y * g_ref[...] * out_scale


def da_step_attention(q, k_new, v_new, cache_k, cache_v, page_table, lam, rel_bias, subln, layer, out_scale):
    nb, n_pages = page_table.shape
    page = cache_v.shape[3]
    n_steps = n_pages // PAGES_PER_STEP
    half = ATTN_ROWS // 2

    def map_rows(a):
        a = jnp.swapaxes(a.reshape(nb, DA_HEADS, 2, HEAD_DIM), 1, 2)
        a = jnp.pad(a, ((0, 0), (0, 0), (0, half - DA_HEADS), (0, 0)))
        return a.reshape(nb, ATTN_ROWS, HEAD_DIM)

    v_rows = jnp.pad(v_new.reshape(nb, DA_HEADS, DA_VDIM), ((0, 0), (0, half - DA_HEADS), (0, 0)))
    v_rows = jnp.concatenate([v_rows, v_rows], axis=1)
    rbt = jnp.zeros((ATTN_ROWS, REL_BUCKETS), F32)
    rbt = rbt.at[0:DA_HEADS].set(rel_bias.T).at[half:half + DA_HEADS].set(rel_bias.T)
    bucket_last = rel_bucket_of_distance(page - jnp.arange(page, dtype=jnp.int32)).reshape(1, page)
    row = pl.BlockSpec((None, ATTN_ROWS, HEAD_DIM), lambda b, p, pt: (b, 0, 0))
    const = lambda shape: pl.BlockSpec(shape, lambda b, p, pt: (0,) * len(shape))
    out = pl.pallas_call(
        functools.partial(_da_step_kernel, n_steps=n_steps, out_scale=out_scale),
        grid_spec=pltpu.PrefetchScalarGridSpec(
            num_scalar_prefetch=1,
            grid=(nb, n_steps),
            in_specs=[pl.BlockSpec(memory_space=pltpu.SMEM), row, row,
                      pl.BlockSpec((None, ATTN_ROWS, DA_VDIM), lambda b, p, pt: (b, 0, 0)),
                      const((ATTN_ROWS, REL_BUCKETS)), const((1, page))]
                     + _page_specs(layer, n_pages, (page * DA_MAPS, HEAD_DIM))
                     + _page_specs(layer, n_pages, (DA_HEADS, page, DA_VDIM)) + [const((1, DA_VDIM))],
            out_specs=pl.BlockSpec((None, half, DA_VDIM), lambda b, p, pt: (b, 0, 0)),
            scratch_shapes=[pltpu.VMEM((ATTN_ROWS, HEAD_DIM), F32), pltpu.VMEM((ATTN_ROWS, HEAD_DIM), F32),
                            pltpu.VMEM((ATTN_ROWS, DA_VDIM), F32), pltpu.VMEM((ATTN_ROWS, page), F32)],
        ),
        out_shape=jax.ShapeDtypeStruct((nb, half, DA_VDIM), F32),
        compiler_params=_params("parallel", "arbitrary"),
    )(page_table, lam.reshape(1), map_rows(q), map_rows(k_new), v_rows, rbt, bucket_last,
      *([cache_k] * PAGES_PER_STEP), *([cache_v] * PAGES_PER_STEP), subln.reshape(1, DA_VDIM))
    return out[:, :DA_HEADS].reshape(nb, DA_WIDTH)


W_IN_SEGMENTS = (2 * GM_WIDTH, SB_WIDTH, SB_WIDTH, SB_WIDTH, DA_WIDTH, DA_WIDTH, DA_WIDTH)


def _lambda(wts, layer):
    lam_init = 0.8 - 0.6 * math.exp(-0.3 * layer)
    lam = (jnp.exp(jnp.sum(wts['da_lq1'][layer] * wts['da_lk1'][layer]))
           - jnp.exp(jnp.sum(wts['da_lq2'][layer] * wts['da_lk2'][layer])) + lam_init)
    return lam.astype(F32), lam_init


def _prompt_layer(x, mem_prompt_bf16, wts, layer, bsz, t, ff_pad, stacked):
    lam, lam_init = _lambda(wts, layer)
    depth = wts['w_in'].shape[0]
    h = rmsnorm(x, wts['norm_mix'][layer], BF16)
    cuts = [0]
    for width in W_IN_SEGMENTS:
        cuts.append(cuts[-1] + width)
    w_in = wts['w_in']
    a_in = matmul(h, w_in, layer, col0=cuts[0], ncols=W_IN_SEGMENTS[0])
    sb_q = matmul_heads(h, w_in, layer, col0=cuts[1], ncols=SB_WIDTH, slab=HEAD_DIM, bsz=bsz, t=t, out_dtype=BF16)
    heads_out = dict(bsz=bsz, t=t, depth=depth, out_layer=layer)
    sb_k = matmul_heads(h, w_in, layer, col0=cuts[2], ncols=SB_WIDTH, slab=HEAD_DIM, stacked=stacked[0], **heads_out)
    sb_v = matmul_heads(h, w_in, layer, col0=cuts[3], ncols=SB_WIDTH, slab=HEAD_DIM, stacked=stacked[1], **heads_out)
    da_q = matmul(h, w_in, layer, col0=cuts[4], ncols=DA_WIDTH, out_dtype=BF16)
    da_k = matmul(h, w_in, layer, col0=cuts[5], ncols=DA_WIDTH)
    da_v = matmul_heads(h, w_in, layer, col0=cuts[6], ncols=DA_WIDTH, slab=DA_VDIM, stacked=stacked[2], **heads_out)

    a_out, _ = gmlp(a_in, wts['gm_ln_g'][layer], wts['gm_ln_b'][layer], wts['gm_ws'][layer],
                    wts['gm_bs'][layer], wts['norm_a'][layer], single_token=False)
    b_out = rmsnorm(sb_attention(sb_q, sb_k, sb_v, layer, bsz, t), wts['norm_b'][layer], BF16)
    c_out = da_attention(da_q, da_k, da_v, layer, lam, wts['rel_bias'], wts['da_subln'][layer], bsz, t,
                         1.0 - lam_init)
    mix = jnp.concatenate([a_out, b_out, c_out], axis=-1)
    x = matmul(mix, wts['w_out'], layer, tn=1024, res=x)

    mem_len = mem_prompt_bf16.shape[0] // bsz
    mk = matmul(mem_prompt_bf16, wts['w_mk'], layer)
    mv = matmul(mem_prompt_bf16, wts['w_mv'], layer)
    hm = rmsnorm(x, wts['norm_mem'][layer], BF16)
    mq = matmul(hm, wts['w_mq'], layer, out_dtype=BF16)
    mo = mem_attention(mq, mk, mv, bsz, t, mem_len)
    x = matmul(mo, wts['w_mo'], layer, tn=1024, res=x)

    hf = rmsnorm(x, wts['norm_ffn'][layer], BF16)
    act, pconv = ffn_gate_up(hf, wts['w_gate'], wts['w_up'], wts['conv_w'], wts['conv_b'], layer, bsz, t)
    x = matmul(act, wts['w_down'], layer, nk=2, res=x)
    return x, (sb_k, sb_v, da_v), (da_k, mk, mv, pconv[:, :, :pconv.shape[2] - ff_pad])


def _pad_rows(a, rows, cols=None):
    cols = a.shape[1] if cols is None else cols
    return jnp.pad(a, ((0, rows - a.shape[0]), (0, cols - a.shape[1])))


def _sample_layer(x, caches, mem_k, mem_v, conv_state, page_table, wts, layer, nb):
    lam, lam_init = _lambda(wts, layer)
    cache_sb_k, cache_sb_v, cache_da_k, cache_da_v = caches
    h = rmsnorm(x, wts['norm_mix'][layer], BF16)
    proj = matmul(h, wts['w_in'], layer, tn=1024)
    cuts = [0]
    for width in W_IN_SEGMENTS:
        cuts.append(cuts[-1] + width)
    a_in, sb_q, sb_k, sb_v, da_q, da_k, da_v = (proj[:, cuts[i]:cuts[i + 1]] for i in range(7))

    a_out, gv = gmlp(a_in, wts['gm_ln_g'][layer], wts['gm_ln_b'][layer], wts['gm_ws'][layer],
                     wts['gm_bs'][layer], wts['norm_a'][layer], single_token=True)
    b_out = sb_step_attention(sb_q[:nb], cache_sb_k, cache_sb_v, page_table, wts['norm_b'][layer], layer)
    c_out = da_step_attention(da_q[:nb], da_k[:nb], da_v[:nb], cache_da_k, cache_da_v, page_table, lam,
                              wts['rel_bias'], wts['da_subln'][layer], layer, 1.0 - lam_init)
    mix = jnp.concatenate([a_out, _pad_rows(b_out, SAMPLE_ROWS).astype(BF16),
                           _pad_rows(c_out, SAMPLE_ROWS).astype(BF16)], axis=-1)
    x = matmul(mix, wts['w_out'], layer, tn=1024, res=x)

    hm = rmsnorm(x, wts['norm_mem'][layer], BF16)
    mq = matmul(hm, wts['w_mq'], layer, out_dtype=BF16)
    mem_len = mem_k.shape[2]
    mq_rep = jnp.repeat(mq[:nb], SAMPLE_ROWS, axis=0)
    mo = mem_attention(mq_rep, mem_k[layer].reshape(nb * mem_len, MEM_WIDTH),
                       mem_v[layer].reshape(nb * mem_len, MEM_WIDTH), nb, SAMPLE_ROWS, mem_len)
    mo = _pad_rows(mo.reshape(nb, SAMPLE_ROWS, MEM_WIDTH)[:, 0], SAMPLE_ROWS)
    x = matmul(mo, wts['w_mo'], layer, tn=1024, res=x)

    hf = rmsnorm(x, wts['norm_ffn'][layer], BF16)
    prefix = conv_state[layer]
    d_ff, d_ffp = prefix.shape[2], wts['w_gate'].shape[2]
    act, g = ffn_gate_up_step(hf, wts['w_gate'], wts['w_up'], wts['conv_w'], wts['conv_b'],
                              _pad_rows(prefix[:, 0], SAMPLE_ROWS, d_ffp),
                              _pad_rows(prefix[:, 1], SAMPLE_ROWS, d_ffp), layer)
    x = matmul(act, wts['w_down'], layer, tn=1024, nk=2, res=x)
    sconv = jnp.stack([prefix[:, 1], g[:nb, :d_ff]], axis=1)
    return x, (sb_k[:nb], sb_v[:nb], da_k[:nb], da_v[:nb], gv[:nb], sconv)


def kernel(x_prompt, x_sample, cache_sb_k, cache_sb_v, cache_da_k, cache_da_v, cache_mem_k, cache_mem_v,
           state_conv, page_table, mem_prompt, norm_mix, w_in, gm_ln_g, gm_ln_b, gm_ws, gm_bs, norm_a, norm_b,
           da_lq1, da_lk1, da_lq2, da_lk2, da_subln, rel_bias, w_out, norm_mem, w_mq, w_mk, w_mv, w_mo,
           norm_ffn, w_gate, conv_w, conv_b, w_up, w_down, norm_final):
    bsz, t, d = x_prompt.shape
    nb = x_sample.shape[0]
    depth = w_in.shape[0]
    n_pool, page = cache_sb_k.shape[1], cache_sb_k.shape[2]
    mem_len = mem_prompt.shape[1]
    d_ff = w_gate.shape[2]
    ff_pad = -d_ff % FFN_TILE
    pad_last = lambda a: jnp.pad(a, ((0, 0),) * (a.ndim - 1) + ((0, ff_pad),))
    wts = {
        'norm_mix': norm_mix, 'w_in': w_in.astype(BF16), 'gm_ln_g': gm_ln_g, 'gm_ln_b': gm_ln_b, 'gm_ws': gm_ws,
        'gm_bs': gm_bs, 'norm_a': norm_a, 'norm_b': norm_b, 'da_lq1': da_lq1, 'da_lk1': da_lk1,
        'da_lq2': da_lq2, 'da_lk2': da_lk2, 'da_subln': da_subln, 'rel_bias': rel_bias,
        'w_out': w_out.astype(BF16), 'norm_mem': norm_mem, 'w_mq': w_mq.astype(BF16),
        'w_mk': w_mk.astype(BF16), 'w_mv': w_mv.astype(BF16), 'w_mo': w_mo.astype(BF16),
        'norm_ffn': norm_ffn, 'w_gate': pad_last(w_gate.astype(BF16)), 'conv_w': pad_last(conv_w),
        'conv_b': pad_last(conv_b), 'w_up': pad_last(w_up.astype(BF16)),
        'w_down': jnp.pad(w_down.astype(BF16), ((0, 0), (0, ff_pad), (0, 0))),
    }
    caches = (jnp.swapaxes(cache_sb_k, 2, 3), jnp.swapaxes(cache_sb_v, 2, 3),
              cache_da_k.reshape(depth, n_pool, page * DA_MAPS, HEAD_DIM), jnp.swapaxes(cache_da_v, 2, 3))
    mem_k = cache_mem_k.reshape(depth, nb, cache_mem_k.shape[2], MEM_WIDTH)
    mem_v = cache_mem_v.reshape(depth, nb, cache_mem_v.shape[2], MEM_WIDTH)
    mem_prompt_bf16 = mem_prompt.reshape(bsz * mem_len, d).astype(BF16)

    xp = x_prompt.reshape(bsz * t, d)
    xs = _pad_rows(x_sample.reshape(nb, d), SAMPLE_ROWS)
    p_new, s_new = [], []
    stacked = (None, None, None)
    for layer in range(depth):
        xp, stacked, p_state = _prompt_layer(xp, mem_prompt_bf16, wts, layer, bsz, t, ff_pad, stacked)
        p_new.append(p_state)
        xs, s_state = _sample_layer(xs, caches, mem_k, mem_v, state_conv, page_table, wts, layer, nb)
        s_new.append(s_state)
    g_final = norm_final
    y_prompt = rmsnorm(xp, g_final, F32).reshape(bsz, t, d)
    y_sample = rmsnorm(xs, g_final, F32)[:nb].reshape(nb, 1, d)

    def stack(states, idx, shape):
        return jnp.stack([s[idx] for s in states]).reshape((depth,) + shape)

    return (
        y_prompt, y_sample,
        jnp.swapaxes(stacked[0], 2, 3), jnp.swapaxes(stacked[1], 2, 3),
        stack(p_new, 0, (bsz, t, DA_HEADS, 2, HEAD_DIM)), jnp.swapaxes(stacked[2], 2, 3),
        stack(p_new, 1, (bsz, mem_len, MEM_HEADS, HEAD_DIM)), stack(p_new, 2, (bsz, mem_len, MEM_HEADS, HEAD_DIM)),
        stack(p_new, 3, (bsz, CONV_W - 1, d_ff)),
        stack(s_new, 0, (nb, 1, SB_HEADS, HEAD_DIM)), stack(s_new, 1, (nb, 1, SB_HEADS, HEAD_DIM)),
        stack(s_new, 2, (nb, 1, DA_HEADS, 2, HEAD_DIM)), stack(s_new, 3, (nb, 1, DA_HEADS, DA_VDIM)),
        stack(s_new, 4, (nb, 1, GM_GROUPS, HEAD_DIM)), stack(s_new, 5, (nb, CONV_W - 1, d_ff)),
    )
```

```python
import functools
import math

import jax
import jax.numpy as jnp
from jax import lax
from jax.experimental import pallas as pl
from jax.experimental.pallas import tpu as pltpu

F32 = jnp.float32
BF16 = jnp.bfloat16

HEAD_DIM = 128
GM_GROUPS = 8
GM_WIDTH = GM_GROUPS * HEAD_DIM
CHUNK = 128
SB_HEADS = 12
SB_WIDTH = SB_HEADS * HEAD_DIM
DA_HEADS = 6
DA_VDIM = 2 * HEAD_DIM
DA_WIDTH = DA_HEADS * DA_VDIM
MEM_HEADS = 4
MEM_WIDTH = MEM_HEADS * HEAD_DIM
REL_BUCKETS = 32
REL_MAX_DIST = 128
CONV_W = 3
ATTN_SCALE = HEAD_DIM ** -0.5
EPS = 1e-6
NEG = -1e30

SAMPLE_ROWS = 16
ATTN_ROWS = 16
VMEM_LIMIT_BYTES = 56 * 1024 * 1024
FFN_TILE = 512
FFN_SPLIT = 2
SB_DECAY_LIMIT = 110.0
STRIP = 32
SB_GROUP = 4
DA_GROUP = 2


def _params(*sem):
    return pltpu.CompilerParams(dimension_semantics=sem, vmem_limit_bytes=VMEM_LIMIT_BYTES)


def _dot(a, b):
    return jnp.dot(a, b, preferred_element_type=F32)


def _dot_nt(a, b):
    return lax.dot_general(a, b, (((1,), (1,)), ((), ())), preferred_element_type=F32)


def _softplus(z):
    return jnp.maximum(z, 0.0) + jnp.log(1.0 + jnp.exp(-jnp.abs(z)))


def _split_bf16(x):
    hi = x.astype(BF16)
    lo = (x - hi.astype(F32)).astype(BF16)
    return hi, lo


def _by_strips(fn, n_out, rows, *arrays):
    outs = [[] for _ in range(n_out)]
    for r0 in range(0, rows, STRIP):
        res = fn(r0, *[a[r0:r0 + STRIP] for a in arrays])
        for acc, r in zip(outs, res):
            acc.append(r)
    return [jnp.concatenate(o, axis=0) for o in outs]


def _rmsnorm_kernel(x_ref, g_ref, o_ref):
    x = x_ref[...].astype(F32)
    y = x * lax.rsqrt(jnp.mean(x * x, axis=-1, keepdims=True) + EPS)
    o_ref[...] = (y * g_ref[...]).astype(o_ref.dtype)


def rmsnorm(x, g, out_dtype, tm=256):
    m, d = x.shape
    tm = min(tm, m)
    return pl.pallas_call(
        _rmsnorm_kernel,
        grid=(m // tm,),
        in_specs=[pl.BlockSpec((tm, d), lambda i: (i, 0)),
                  pl.BlockSpec((1, d), lambda i: (0, 0))],
        out_specs=pl.BlockSpec((tm, d), lambda i: (i, 0)),
        out_shape=jax.ShapeDtypeStruct((m, d), out_dtype),
        compiler_params=_params("parallel"),
    )(x, g.reshape(1, d).astype(F32))


def _mm_kernel(*refs, nk, has_res):
    x_ref, w_ref = refs[0], refs[1]
    r_ref = refs[2] if has_res else None
    o_ref = refs[2 + has_res]
    part = _dot(x_ref[...], w_ref[...])
    if nk == 1:
        if has_res:
            part = part + r_ref[...]
        o_ref[...] = part.astype(o_ref.dtype)
        return
    acc_ref = refs[3 + has_res]
    k = pl.program_id(2)

    @pl.when(k == 0)
    def _():
        acc_ref[...] = part

    @pl.when(k > 0)
    def _():
        acc_ref[...] += part

    @pl.when(k == nk - 1)
    def _():
        r = acc_ref[...]
        if has_res:
            r = r + r_ref[...]
        o_ref[...] = r.astype(o_ref.dtype)


def matmul(x, w, layer, *, col0=0, ncols=None, tm=1024, tn=512, nk=1, res=None, out_dtype=F32):
    m, kdim = x.shape
    n_total = w.shape[2]
    ncols = n_total if ncols is None else ncols
    tm = min(tm, m)
    tn = min(tn, ncols)
    while ncols % tn or col0 % tn:
        tn //= 2
    tk = kdim // nk
    jb = col0 // tn
    in_specs = [pl.BlockSpec((tm, tk), lambda i, j, k: (i, k)),
                pl.BlockSpec((None, tk, tn), lambda i, j, k: (layer, k, jb + j))]
    args = [x, w]
    if res is not None:
        in_specs.append(pl.BlockSpec((tm, tn), lambda i, j, k: (i, j)))
        args.append(res)
    scratch = [pltpu.VMEM((tm, tn), F32)] if nk > 1 else []
    return pl.pallas_call(
        functools.partial(_mm_kernel, nk=nk, has_res=res is not None),
        grid=(m // tm, ncols // tn, nk),
        in_specs=in_specs,
        out_specs=pl.BlockSpec((tm, tn), lambda i, j, k: (i, j)),
        out_shape=jax.ShapeDtypeStruct((m, ncols), out_dtype),
        scratch_shapes=scratch,
        compiler_params=_params("parallel", "parallel", "arbitrary"),
    )(*args)


def _mm_heads_kernel(*refs, slab):
    x_ref, w_ref, o_ref = refs[0], refs[1], refs[-1]
    part = _dot(x_ref[...], w_ref[...])
    for s in range(o_ref.shape[0]):
        o_ref[s] = part[:, s * slab:(s + 1) * slab].astype(o_ref.dtype)


def matmul_heads(x, w, layer, *, col0, ncols, slab, bsz, t, depth=1, out_layer=0, stacked=None,
                 tm=1024, tn=512, out_dtype=F32):
    m, kdim = x.shape
    tm = min(tm, t)
    tn = min(tn, ncols)
    jb = col0 // tn
    tiles_per_seq = t // tm
    in_specs = [pl.BlockSpec((tm, kdim), lambda i, j: (i, 0)),
                pl.BlockSpec((None, kdim, tn), lambda i, j: (layer, 0, jb + j))]
    args = [x, w]
    aliases = {}
    if stacked is not None:
        in_specs.append(pl.BlockSpec(memory_space=pl.ANY))
        args.append(stacked)
        aliases = {2: 0}
    return pl.pallas_call(
        functools.partial(_mm_heads_kernel, slab=slab),
        grid=(m // tm, ncols // tn),
        in_specs=in_specs,
        out_specs=pl.BlockSpec((None, None, tn // slab, tm, slab),
                               lambda i, j: (out_layer, i // tiles_per_seq, j, i % tiles_per_seq, 0)),
        out_shape=jax.ShapeDtypeStruct((depth, bsz, ncols // slab, t, slab), out_dtype),
        input_output_aliases=aliases,
        compiler_params=_params("parallel", "parallel"),
    )(*args)


def _mm_token_rows_kernel(*refs, n_w):
    x_ref, w_refs = refs[0], refs[1:1 + n_w]
    std_ref, rows_ref = refs[-2], refs[-1]
    tm = x_ref.shape[0]
    tn = w_refs[0].shape[1]
    per_w = tn // HEAD_DIM
    n_blocks = n_w * per_w
    x = x_ref[...]
    for s in range(n_w):
        part = _dot(x, w_refs[s][...])
        std_ref[:, s * tn:(s + 1) * tn] = part.astype(std_ref.dtype)
        for c in range(per_w):
            rows_ref[pl.ds(s * per_w + c, tm, stride=n_blocks), :] = part[:, c * HEAD_DIM:(c + 1) * HEAD_DIM]


def matmul_token_rows(x, w, layer, *, col0, ncols, bsz, t, depth, out_layer, stacked, tm=512, tn=512):
    m, kdim = x.shape
    tm = min(tm, t)
    n_w = ncols // tn
    n_blocks = ncols // HEAD_DIM
    tiles_per_seq = t // tm
    in_specs = [pl.BlockSpec((tm, kdim), lambda i: (i, 0))]
    in_specs += [pl.BlockSpec((None, kdim, tn), lambda i, s=s: (layer, 0, col0 // tn + s)) for s in range(n_w)]
    args = [x] + [w] * n_w
    aliases = {}
    if stacked is not None:
        in_specs.append(pl.BlockSpec(memory_space=pl.ANY))
        args.append(stacked)
        aliases = {1 + n_w: 1}
    return pl.pallas_call(
        functools.partial(_mm_token_rows_kernel, n_w=n_w),
        grid=(m // tm,),
        in_specs=in_specs,
        out_specs=[pl.BlockSpec((tm, ncols), lambda i: (i, 0)),
                   pl.BlockSpec((None, None, tm * n_blocks, HEAD_DIM),
                                lambda i: (out_layer, i // tiles_per_seq, i % tiles_per_seq, 0))],
        out_shape=[jax.ShapeDtypeStruct((m, ncols), BF16),
                   jax.ShapeDtypeStruct((depth, bsz, t * n_blocks, HEAD_DIM), F32)],
        input_output_aliases=aliases,
        compiler_params=_params("parallel"),
    )(*args)


def _gmlp_kernel(a_ref, lng_ref, lnb_ref, ws_ref, bs_ref, na_ref, o_ref, gv_ref, *, single_token):
    uv = jax.nn.gelu(a_ref[...])
    if not single_token:
        r = lax.broadcasted_iota(jnp.int32, (CHUNK, CHUNK), 0)
        c = lax.broadcasted_iota(jnp.int32, (CHUNK, CHUNK), 1)
        tril = r >= c
    outs = []
    for g in range(GM_GROUPS):
        lo, hi = g * HEAD_DIM, (g + 1) * HEAD_DIM
        u = uv[:, lo:hi]
        v = uv[:, GM_WIDTH + lo:GM_WIDTH + hi]
        vc = v - jnp.mean(v, axis=-1, keepdims=True)
        y = vc * lax.rsqrt(jnp.mean(vc * vc, axis=-1, keepdims=True) + EPS)
        gv = y * lng_ref[:, lo:hi] + lnb_ref[:, lo:hi]
        gv_ref[:, lo:hi] = gv
        if single_token:
            mix = ws_ref[:, lo:hi] * gv + bs_ref[:, lo:hi]
        else:
            wm = jnp.where(tril, ws_ref[g], 0.0).astype(BF16)
            mix = _dot(wm, gv.astype(BF16)) + bs_ref[:, g:g + 1]
        outs.append(u * mix)
    t = jnp.concatenate(outs, axis=-1)
    y = t * lax.rsqrt(jnp.mean(t * t, axis=-1, keepdims=True) + EPS)
    o_ref[...] = (y * na_ref[...]).astype(o_ref.dtype)


def gmlp(a_in, ln_g, ln_b, ws, bs, norm_a, *, single_token):
    m = a_in.shape[0]
    rows = m if single_token else CHUNK
    full = lambda shape: pl.BlockSpec(shape, lambda i: (0,) * len(shape))
    if single_token:
        ws_arg = jnp.repeat(ws[:, 0, 0], HEAD_DIM).reshape(1, GM_WIDTH)
        bs_arg = jnp.repeat(bs[:, 0], HEAD_DIM).reshape(1, GM_WIDTH)
    else:
        ws_arg = ws
        bs_arg = bs.T
    return pl.pallas_call(
        functools.partial(_gmlp_kernel, single_token=single_token),
        grid=(m // rows,),
        in_specs=[pl.BlockSpec((rows, 2 * GM_WIDTH), lambda i: (i, 0)),
                  full((1, GM_WIDTH)), full((1, GM_WIDTH)),
                  full(ws_arg.shape), full(bs_arg.shape), full((1, GM_WIDTH))],
        out_specs=[pl.BlockSpec((rows, GM_WIDTH), lambda i: (i, 0)),
                   pl.BlockSpec((rows, GM_WIDTH), lambda i: (i, 0))],
        out_shape=[jax.ShapeDtypeStruct((m, GM_WIDTH), BF16),
                   jax.ShapeDtypeStruct((m, GM_WIDTH), F32)],
        compiler_params=_params("parallel"),
    )(a_in, ln_g.reshape(1, GM_WIDTH), ln_b.reshape(1, GM_WIDTH), ws_arg, bs_arg,
      norm_a.reshape(1, GM_WIDTH))


def _sb_kernel(q_ref, k_ref, v_ref, o_ref, kb_ref, vb_ref, *, t, tq):
    group = q_ref.shape[0]
    heads = range(group)
    kb_ref[...] = k_ref[...].astype(BF16)
    vb_ref[...] = v_ref[...].astype(BF16)
    r = lax.broadcasted_iota(jnp.int32, (tq, tq), 0)
    c = lax.broadcasted_iota(jnp.int32, (tq, tq), 1)
    later = jnp.where(r > c, 1.0, 0.0).astype(BF16)
    later2 = jnp.concatenate([later, later], axis=0)
    srow = lax.broadcasted_iota(jnp.int32, (STRIP, tq), 0)
    scol = lax.broadcasted_iota(jnp.int32, (STRIP, tq), 1)

    def tile(qb, kj, state, masked):
        start = pl.multiple_of(kj * tq, tq)
        z = [_dot_nt(qb[g], kb_ref[g, pl.ds(start, tq), :]) for g in heads]

        def front(r0, zs):
            t = zs * ATTN_SCALE
            p = _softplus(t)
            tmp = t - p
            if masked:
                p = jnp.where(scol < srow + r0, p, 0.0)
            hi, lo = _split_bf16(p)
            return jnp.concatenate([hi, lo], axis=1), tmp, p[:, 0:1]

        fr = [_by_strips(front, 3, tq, z[g]) for g in heads]
        cum = [_dot(fr[g][0], later2) for g in heads]

        def back(r0, tmp, cm, run):
            a = jnp.exp(tmp - (run + cm))
            if masked:
                a = jnp.where(scol < srow + r0, a, 0.0)
            return (a.astype(BF16),)

        a = [_by_strips(back, 1, tq, fr[g][1], cum[g], state[g][0])[0] for g in heads]
        pv = [_dot(a[g], vb_ref[g, pl.ds(start, tq), :]) for g in heads]
        return tuple((state[g][0] + cum[g][:, 0:1] + fr[g][2], state[g][1] + pv[g]) for g in heads)

    def q_body(qi, carry):
        qstart = pl.multiple_of(qi * tq, tq)
        qb = [q_ref[g, pl.ds(qstart, tq), :] for g in heads]
        init = tuple((jnp.zeros((tq, 1), F32), jnp.zeros((tq, HEAD_DIM), F32)) for _ in heads)
        state = tile(qb, qi, init, True)

        def cond(carry):
            j, st = carry
            low = st[0][0]
            for g in heads[1:]:
                low = jnp.minimum(low, st[g][0])
            return jnp.logical_and(j < qi, jnp.min(low) < SB_DECAY_LIMIT)

        _, state = lax.while_loop(cond, lambda cr: (cr[0] + 1, tile(qb, qi - 1 - cr[0], cr[1], False)),
                                  (jnp.int32(0), state))
        o_ref[pl.ds(qstart, tq), :] = jnp.concatenate([acc for _, acc in state], axis=1)
        return carry

    lax.fori_loop(0, t // tq, q_body, 0)


def _attn_tq(t):
    return 256 if t % 256 == 0 and t >= 1024 else 128


def sb_attention(q, k, v, layer, bsz, t):
    tq = _attn_tq(t)
    group = SB_GROUP

    def spec(lead):
        return pl.BlockSpec((None, None, group, t, HEAD_DIM), lambda b, h: (lead, b, h, 0, 0))

    return pl.pallas_call(
        functools.partial(_sb_kernel, t=t, tq=tq),
        grid=(bsz, SB_HEADS // group),
        in_specs=[spec(0), spec(layer), spec(layer)],
        out_specs=pl.BlockSpec((t, group * HEAD_DIM), lambda b, h: (b, h)),
        out_shape=jax.ShapeDtypeStruct((bsz * t, SB_WIDTH), F32),
        scratch_shapes=[pltpu.VMEM((group, t, HEAD_DIM), BF16), pltpu.VMEM((group, t, HEAD_DIM), BF16)],
        compiler_params=_params("parallel", "parallel"),
    )(q, k, v)


def rel_bucket_of_distance(n):
    max_exact = REL_BUCKETS // 2
    nf = jnp.maximum(n, 1).astype(F32)
    large = max_exact + (jnp.log(nf / max_exact) / math.log(REL_MAX_DIST / max_exact)
                         * (REL_BUCKETS - max_exact)).astype(jnp.int32)
    large = jnp.minimum(large, REL_BUCKETS - 1)
    return jnp.where(n < max_exact, n, large)


def _bias_from_buckets(buckets, rb_ref, head):
    bias = jnp.zeros(buckets.shape, F32)
    for b in range(REL_BUCKETS):
        bias = jnp.where(buckets == b, rb_ref[b, head], bias)
    return bias


def _da_kernel(lam_ref, rb_ref, q_ref, k_ref, v_ref, bd_ref, bo_ref, g_ref, o_ref,
               kb_ref, vb_ref, biasd_ref, biaso_ref, *, t, tq, out_scale):
    group = v_ref.shape[0]
    head0 = pl.program_id(1) * group
    lam = lam_ref[0]
    chains = [(g, mi) for g in range(group) for mi in range(2)]
    cols = [g * DA_VDIM + mi * HEAD_DIM for g, mi in chains]
    kb_ref[...] = k_ref[...].astype(BF16)
    vb_ref[...] = v_ref[...].astype(BF16)
    for g in range(group):
        biasd_ref[g] = _bias_from_buckets(bd_ref[...], rb_ref, head0 + g)
        biaso_ref[g] = _bias_from_buckets(bo_ref[...], rb_ref, head0 + g)
    bias_far = [rb_ref[REL_BUCKETS - 1, head0 + g] for g in range(group)]
    r = lax.broadcasted_iota(jnp.int32, (tq, tq), 0)
    c = lax.broadcasted_iota(jnp.int32, (tq, tq), 1)
    causal = c <= r

    def tile(qb, kj, state, bias, masked):
        start = pl.multiple_of(kj * tq, tq)
        kblk = kb_ref[pl.ds(start, tq), :]
        n = range(len(chains))
        s = [_dot_nt(qb[:, cols[i]:cols[i] + HEAD_DIM], kblk[:, cols[i]:cols[i] + HEAD_DIM]) * ATTN_SCALE
             + bias[chains[i][0]] for i in n]
        if masked:
            s = [jnp.where(causal, s[i], NEG) for i in n]
        m_new = [jnp.maximum(state[i][0], jnp.max(s[i], axis=-1, keepdims=True)) for i in n]
        alpha = [jnp.exp(state[i][0] - m_new[i]) for i in n]
        p = [jnp.exp(s[i] - m_new[i]) for i in n]
        l_new = [alpha[i] * state[i][1] + jnp.sum(p[i], axis=-1, keepdims=True) for i in n]
        pv = [_dot(p[i].astype(BF16), vb_ref[chains[i][0], pl.ds(start, tq), :]) for i in n]
        return tuple((m_new[i], l_new[i], alpha[i] * state[i][2] + pv[i]) for i in n)

    def q_body(qi, carry):
        qstart = pl.multiple_of(qi * tq, tq)
        qb = q_ref[pl.ds(qstart, tq), :]
        init = tuple((jnp.full((tq, 1), NEG, F32), jnp.zeros((tq, 1), F32), jnp.zeros((tq, DA_VDIM), F32))
                     for _ in chains)
        state = tile(qb, qi, init, [biasd_ref[g] for g in range(group)], True)

        def k_body(j, st):
            bias = [jnp.where(j == 0, biaso_ref[g], bias_far[g]) for g in range(group)]
            return tile(qb, qi - 1 - j, st, bias, False)

        state = lax.fori_loop(0, qi, k_body, state)
        outs = []
        for g in range(group):
            (_, l1, a1), (_, l2, a2) = state[2 * g], state[2 * g + 1]
            cvec = a1 / l1 - lam * (a2 / l2)
            y = cvec * lax.rsqrt(jnp.mean(cvec * cvec, axis=-1, keepdims=True) + EPS)
            outs.append(y * g_ref[...] * out_scale)
        o_ref[pl.ds(qstart, tq), :] = jnp.concatenate(outs, axis=1).astype(o_ref.dtype)
        return carry

    lax.fori_loop(0, t // tq, q_body, 0)


def da_attention(q, k, v, layer, lam, rel_bias, subln, bsz, t, out_scale):
    tq = _attn_tq(t)
    group = DA_GROUP
    pos = jnp.arange(tq, dtype=jnp.int32)
    dist = pos[:, None] - pos[None, :]
    bucket_diag = rel_bucket_of_distance(jnp.maximum(dist, 0))
    bucket_off = rel_bucket_of_distance(dist + tq)
    spec = pl.BlockSpec((t, group * DA_VDIM), lambda b, h: (b, h))
    smem = pl.BlockSpec(memory_space=pltpu.SMEM)
    const = lambda shape: pl.BlockSpec(shape, lambda b, h: (0,) * len(shape))
    return pl.pallas_call(
        functools.partial(_da_kernel, t=t, tq=tq, out_scale=out_scale),
        grid=(bsz, DA_HEADS // group),
        in_specs=[smem, smem, spec, spec,
                  pl.BlockSpec((None, None, group, t, DA_VDIM), lambda b, h: (layer, b, h, 0, 0)),
                  const((tq, tq)), const((tq, tq)), const((1, DA_VDIM))],
        out_specs=spec,
        out_shape=jax.ShapeDtypeStruct((bsz * t, DA_WIDTH), BF16),
        scratch_shapes=[pltpu.VMEM((t, group * DA_VDIM), BF16), pltpu.VMEM((group, t, DA_VDIM), BF16),
                        pltpu.VMEM((group, tq, tq), F32), pltpu.VMEM((group, tq, tq), F32)],
        compiler_params=_params("parallel", "parallel"),
    )(lam.reshape(1), rel_bias, q, k, v, bucket_diag, bucket_off, subln.reshape(1, DA_VDIM))


def _mem_kernel(q_ref, k_ref, v_ref, o_ref):
    outs = []
    for h in range(MEM_HEADS):
        lo, hi = h * HEAD_DIM, (h + 1) * HEAD_DIM
        s = _dot_nt(q_ref[:, lo:hi], k_ref[:, lo:hi].astype(BF16)) * ATTN_SCALE
        p = jnp.exp(s - jnp.max(s, axis=-1, keepdims=True))
        den = jnp.sum(p, axis=-1, keepdims=True)
        outs.append(_dot(p.astype(BF16), v_ref[:, lo:hi].astype(BF16)) / den)
    o_ref[...] = jnp.concatenate(outs, axis=-1).astype(o_ref.dtype)


def mem_attention(q, mk, mv, bsz, t, mem_len, tq=512):
    tq = min(tq, t)
    nq = t // tq
    kv_spec = pl.BlockSpec((mem_len, MEM_WIDTH), lambda b, i: (b, 0))
    return pl.pallas_call(
        _mem_kernel,
        grid=(bsz, nq),
        in_specs=[pl.BlockSpec((tq, MEM_WIDTH), lambda b, i: (b * nq + i, 0)), kv_spec, kv_spec],
        out_specs=pl.BlockSpec((tq, MEM_WIDTH), lambda b, i: (b * nq + i, 0)),
        out_shape=jax.ShapeDtypeStruct((bsz * t, MEM_WIDTH), BF16),
        compiler_params=_params("parallel", "parallel"),
    )(q, mk, mv)


def _silu(x):
    return x / (1.0 + jnp.exp(-x))


def _ffn_kernel(h_ref, wg_ref, wu_ref, cw_ref, cb_ref, act_ref, pc_ref, gbuf_ref, *, tm, tiles_per_seq):
    i = pl.program_id(1)
    first = (i % tiles_per_seq) == 0
    h = h_ref[...]
    sub = act_ref.shape[1] // FFN_SPLIT
    for c in range(FFN_SPLIT):
        cols = slice(c * sub, (c + 1) * sub)
        buf = gbuf_ref.at[c]
        g = _dot(h, wg_ref[:, cols])
        u = _dot(h, wu_ref[:, cols])
        buf[0:8, :] = jnp.where(first, 0.0, buf[tm:tm + 8, :])
        buf[8:tm + 8, :] = g
        g1 = buf[7:tm + 7, :]
        g2 = buf[6:tm + 6, :]
        gc = cb_ref[:, cols] + cw_ref[0:1, cols] * g2 + cw_ref[1:2, cols] * g1 + cw_ref[2:3, cols] * g
        act_ref[:, cols] = (_silu(gc) * u).astype(act_ref.dtype)
        pc_ref[:, cols] = g[tm - (CONV_W - 1):tm, :]


def ffn_gate_up(h, wg, wu, conv_w, conv_b, layer, bsz, t, *, tm=1024, tn=512):
    m, d = h.shape
    d_ff = wg.shape[2]
    tm = min(tm, t)
    while d_ff % tn:
        tn //= 2
    tiles_per_seq = t // tm
    wspec = pl.BlockSpec((None, d, tn), lambda j, i: (layer, 0, j))
    return pl.pallas_call(
        functools.partial(_ffn_kernel, tm=tm, tiles_per_seq=tiles_per_seq),
        grid=(d_ff // tn, m // tm),
        in_specs=[pl.BlockSpec((tm, d), lambda j, i: (i, 0)), wspec, wspec,
                  pl.BlockSpec((None, CONV_W, tn), lambda j, i: (layer, 0, j)),
                  pl.BlockSpec((None, 1, tn), lambda j, i: (layer, 0, j))],
        out_specs=[pl.BlockSpec((tm, tn), lambda j, i: (i, j)),
                   pl.BlockSpec((None, CONV_W - 1, tn), lambda j, i: (i // tiles_per_seq, 0, j))],
        out_shape=[jax.ShapeDtypeStruct((m, d_ff), BF16),
                   jax.ShapeDtypeStruct((bsz, CONV_W - 1, d_ff), F32)],
        scratch_shapes=[pltpu.VMEM((FFN_SPLIT, tm + 8, tn // FFN_SPLIT), F32)],
        compiler_params=_params("parallel", "arbitrary"),
    )(h, wg, wu, conv_w, conv_b.reshape(conv_b.shape[0], 1, d_ff))


def _ffn_step_kernel(h_ref, wg_ref, wu_ref, cw_ref, cb_ref, p0_ref, p1_ref, act_ref, g_ref):
    h = h_ref[...]
    g = _dot(h, wg_ref[...])
    u = _dot(h, wu_ref[...])
    gc = cb_ref[...] + cw_ref[0:1, :] * p0_ref[...] + cw_ref[1:2, :] * p1_ref[...] + cw_ref[2:3, :] * g
    act_ref[...] = (_silu(gc) * u).astype(act_ref.dtype)
    g_ref[...] = g


def ffn_gate_up_step(h, wg, wu, conv_w, conv_b, prefix0, prefix1, layer, *, tn=512):
    m, d = h.shape
    d_ff = wg.shape[2]
    while d_ff % tn:
        tn //= 2
    wspec = pl.BlockSpec((None, d, tn), lambda j: (layer, 0, j))
    row = pl.BlockSpec((m, tn), lambda j: (0, j))
    return pl.pallas_call(
        _ffn_step_kernel,
        grid=(d_ff // tn,),
        in_specs=[pl.BlockSpec((m, d), lambda j: (0, 0)), wspec, wspec,
                  pl.BlockSpec((None, CONV_W, tn), lambda j: (layer, 0, j)),
                  pl.BlockSpec((None, 1, tn), lambda j: (layer, 0, j)), row, row],
        out_specs=[row, row],
        out_shape=[jax.ShapeDtypeStruct((m, d_ff), BF16), jax.ShapeDtypeStruct((m, d_ff), F32)],
        compiler_params=_params("parallel"),
    )(h, wg, wu, conv_w, conv_b.reshape(conv_b.shape[0], 1, d_ff), prefix0, prefix1)


PAGES_PER_STEP = 8
DA_MAPS = 2 * DA_HEADS


def _page_specs(layer, n_pages, block):
    specs = []
    for s in range(PAGES_PER_STEP):
        def imap(b, p, pt, s=s):
            return (layer, pt[b, n_pages - 1 - (p * PAGES_PER_STEP + s)]) + (0,) * len(block)
        specs.append(pl.BlockSpec((None, None) + block, imap))
    return specs


def _sb_step_kernel(pt_ref, q_ref, kc_ref, vc_ref, g_ref, o_ref, kbuf_ref, vbuf_ref, sem_ref, *, layer, n_pages):
    b = pl.program_id(0)
    page = kbuf_ref.shape[2]
    rowid = lax.broadcasted_iota(jnp.int32, (ATTN_ROWS, HEAD_DIM), 0)
    q = q_ref[...]
    q_rows = [jnp.where(rowid == h, q, 0.0).astype(BF16) for h in range(SB_HEADS)]
    r = lax.broadcasted_iota(jnp.int32, (page, page), 0)
    c = lax.broadcasted_iota(jnp.int32, (page, page), 1)
    later = jnp.where(r > c, 1.0, 0.0).astype(BF16)
    later2 = jnp.concatenate([later, later], axis=0)
    half = SB_HEADS // 2

    def page_copies(i, slot):
        pg = pt_ref[b, n_pages - 1 - i]
        return (pltpu.make_async_copy(kc_ref.at[layer, pg], kbuf_ref.at[slot], sem_ref.at[0, slot]),
                pltpu.make_async_copy(vc_ref.at[layer, pg], vbuf_ref.at[slot], sem_ref.at[1, slot]))

    def not_decayed(run):
        return jnp.min(jnp.where(rowid[:, 0:1] < SB_HEADS, run, SB_DECAY_LIMIT)) < SB_DECAY_LIMIT

    def cond(carry):
        i, run, _ = carry
        return jnp.logical_and(i < n_pages, not_decayed(run))

    def body(carry):
        i, run, acc = carry
        slot = i % 2
        for cp in page_copies(i, slot):
            cp.wait()

        @pl.when(i + 1 < n_pages)
        def _():
            for cp in page_copies(i + 1, 1 - slot):
                cp.start()

        parts = []
        for h0 in (0, half):
            zp = _dot_nt(q_rows[h0], kbuf_ref[slot, h0].astype(BF16))
            for h in range(h0 + 1, h0 + half):
                zp = zp + _dot_nt(q_rows[h], kbuf_ref[slot, h].astype(BF16))
            parts.append(zp)
        t = (parts[0] + parts[1]) * ATTN_SCALE
        p = _softplus(t)
        hi, lo = _split_bf16(p)
        cum = _dot(jnp.concatenate([hi, lo], axis=1), later2)
        a = jnp.exp((t - p) - (run + cum)).astype(BF16)
        for h in range(SB_HEADS):
            acc = acc + jnp.where(rowid == h, _dot(a, vbuf_ref[slot, h].astype(BF16)), 0.0)
        return i + 1, run + jnp.sum(p, axis=-1, keepdims=True), acc

    for cp in page_copies(0, 0):
        cp.start()
    init = (jnp.int32(0), jnp.zeros((ATTN_ROWS, 1), F32), jnp.zeros((ATTN_ROWS, HEAD_DIM), F32))
    done, _, acc = lax.while_loop(cond, body, init)

    @pl.when(done < n_pages)
    def _():
        for cp in page_copies(done, done % 2):
            cp.wait()

    ms = jnp.sum(jnp.sum(acc * acc, axis=-1, keepdims=True), axis=0, keepdims=True) / SB_WIDTH
    o_ref[...] = acc * lax.rsqrt(ms + EPS) * g_ref[...]


def sb_step_attention(q, cache_k, cache_v, page_table, norm_b, layer):
    nb, n_pages = page_table.shape
    page = cache_k.shape[3]
    pad_heads = ((0, 0), (0, ATTN_ROWS - SB_HEADS), (0, 0))
    q_rows = jnp.pad(q.reshape(nb, SB_HEADS, HEAD_DIM), pad_heads)
    gain = jnp.pad(norm_b.reshape(SB_HEADS, HEAD_DIM), pad_heads[1:])
    row = pl.BlockSpec((None, ATTN_ROWS, HEAD_DIM), lambda b, pt: (b, 0, 0))
    hbm = pl.BlockSpec(memory_space=pl.ANY)
    out = pl.pallas_call(
        functools.partial(_sb_step_kernel, layer=layer, n_pages=n_pages),
        grid_spec=pltpu.PrefetchScalarGridSpec(
            num_scalar_prefetch=1,
            grid=(nb,),
            in_specs=[row, hbm, hbm, pl.BlockSpec((ATTN_ROWS, HEAD_DIM), lambda b, pt: (0, 0))],
            out_specs=row,
            scratch_shapes=[pltpu.VMEM((2, SB_HEADS, page, HEAD_DIM), F32),
                            pltpu.VMEM((2, SB_HEADS, page, HEAD_DIM), F32),
                            pltpu.SemaphoreType.DMA((2, 2))],
        ),
        out_shape=jax.ShapeDtypeStruct((nb, ATTN_ROWS, HEAD_DIM), F32),
        compiler_params=_params("arbitrary"),
    )(page_table, q_rows, cache_k, cache_v, gain)
    return out[:, :SB_HEADS].reshape(nb, SB_WIDTH)


def _da_step_kernel(pt_ref, lam_ref, q_ref, kn_ref, vn_ref, rbt_ref, bl_ref, *refs, n_steps, out_scale):
    del pt_ref
    k_refs = refs[:PAGES_PER_STEP]
    v_refs = refs[PAGES_PER_STEP:2 * PAGES_PER_STEP]
    g_ref = refs[2 * PAGES_PER_STEP]
    o_ref = refs[2 * PAGES_PER_STEP + 1]
    m_ref, l_ref, acc_ref, blast_ref = refs[2 * PAGES_PER_STEP + 2:]
    p = pl.program_id(1)
    half = ATTN_ROWS // 2
    page = v_refs[0].shape[1]
    rowid = lax.broadcasted_iota(jnp.int32, (ATTN_ROWS, HEAD_DIM), 0)
    vrow = lax.broadcasted_iota(jnp.int32, (ATTN_ROWS, DA_VDIM), 0) % half
    q = q_ref[...]
    q_rows = [jnp.where(rowid == half * (j % 2) + j // 2, q, 0.0).astype(BF16) for j in range(DA_MAPS)]
    bias_far = rbt_ref[:, REL_BUCKETS - 1:REL_BUCKETS]

    @pl.when(p == 0)
    def _():
        prod = q.astype(BF16).astype(F32) * kn_ref[...].astype(BF16).astype(F32)
        s_self = jnp.sum(prod, axis=-1, keepdims=True) * ATTN_SCALE + rbt_ref[:, 0:1]
        m_ref[...] = jnp.broadcast_to(s_self, m_ref.shape)
        l_ref[...] = jnp.ones(l_ref.shape, F32)
        acc_ref[...] = vn_ref[...].astype(BF16).astype(F32)
        bias = jnp.zeros(blast_ref.shape, F32)
        for b in range(REL_BUCKETS):
            bias = jnp.where(bl_ref[...] == b, rbt_ref[:, b:b + 1], bias)
        blast_ref[...] = bias

    m_run = m_ref[:, 0:1]
    l_run = l_ref[:, 0:1]
    acc = acc_ref[...]
    scs = []
    for s in range(PAGES_PER_STEP):
        bias = bias_far
        if s == 0:
            bias = jnp.where(p == 0, blast_ref[...], bias_far)
        parts = []
        for j0 in (0, DA_HEADS):
            part = None
            for j in range(j0, j0 + DA_HEADS):
                kj = k_refs[s][pl.ds(j, page, stride=DA_MAPS), :].astype(BF16)
                d = _dot_nt(q_rows[j], kj)
                part = d if part is None else part + d
            parts.append(part)
        scs.append((parts[0] + parts[1]) * ATTN_SCALE + bias)
    m_new = m_run
    for sc in scs:
        m_new = jnp.maximum(m_new, jnp.max(sc, axis=-1, keepdims=True))
    alpha = jnp.exp(m_run - m_new)
    l_run = alpha * l_run
    acc_parts = []
    for s in range(PAGES_PER_STEP):
        pr = jnp.exp(scs[s] - m_new)
        l_run = l_run + jnp.sum(pr, axis=-1, keepdims=True)
        pb = pr.astype(BF16)
        part = jnp.zeros((ATTN_ROWS, DA_VDIM), F32)
        for h in range(DA_HEADS):
            part = part + jnp.where(vrow == h, _dot(pb, v_refs[s][h].astype(BF16)), 0.0)
        acc_parts.append(part)
    acc = alpha * acc + sum(acc_parts)
    m_run = m_new
    m_ref[...] = jnp.broadcast_to(m_run, m_ref.shape)
    l_ref[...] = jnp.broadcast_to(l_run, l_ref.shape)
    acc_ref[...] = acc

    @pl.when(p == n_steps - 1)
    def _():
        out = acc / l_run
        cvec = out[0:half, :] - lam_ref[0] * out[half:ATTN_ROWS, :]
        y = cvec * lax.rsqrt(jnp.mean(cvec * cvec, axis=-1, keepdims=True) + EPS)
        o_ref[...] = y * g_ref[...] * out_scale


def da_step_attention(q, k_new, v_new, cache_k, cache_v, page_table, lam, rel_bias, subln, layer, out_scale):
    nb, n_pages = page_table.shape
    page = cache_v.shape[3]
    n_steps = n_pages // PAGES_PER_STEP
    half = ATTN_ROWS // 2

    def map_rows(a):
        a = jnp.swapaxes(a.reshape(nb, DA_HEADS, 2, HEAD_DIM), 1, 2)
        a = jnp.pad(a, ((0, 0), (0, 0), (0, half - DA_HEADS), (0, 0)))
        return a.reshape(nb, ATTN_ROWS, HEAD_DIM)

    v_rows = jnp.pad(v_new.reshape(nb, DA_HEADS, DA_VDIM), ((0, 0), (0, half - DA_HEADS), (0, 0)))
    v_rows = jnp.concatenate([v_rows, v_rows], axis=1)
    rbt = jnp.zeros((ATTN_ROWS, REL_BUCKETS), F32)
    rbt = rbt.at[0:DA_HEADS].set(rel_bias.T).at[half:half + DA_HEADS].set(rel_bias.T)
    bucket_last = rel_bucket_of_distance(page - jnp.arange(page, dtype=jnp.int32)).reshape(1, page)
    row = pl.BlockSpec((None, ATTN_ROWS, HEAD_DIM), lambda b, p, pt: (b, 0, 0))
    const = lambda shape: pl.BlockSpec(shape, lambda b, p, pt: (0,) * len(shape))
    out = pl.pallas_call(
        functools.partial(_da_step_kernel, n_steps=n_steps, out_scale=out_scale),
        grid_spec=pltpu.PrefetchScalarGridSpec(
            num_scalar_prefetch=1,
            grid=(nb, n_steps),
            in_specs=[pl.BlockSpec(memory_space=pltpu.SMEM), row, row,
                      pl.BlockSpec((None, ATTN_ROWS, DA_VDIM), lambda b, p, pt: (b, 0, 0)),
                      const((ATTN_ROWS, REL_BUCKETS)), const((1, page))]
                     + _page_specs(layer, n_pages, (page * DA_MAPS, HEAD_DIM))
                     + _page_specs(layer, n_pages, (DA_HEADS, page, DA_VDIM)) + [const((1, DA_VDIM))],
            out_specs=pl.BlockSpec((None, half, DA_VDIM), lambda b, p, pt: (b, 0, 0)),
            scratch_shapes=[pltpu.VMEM((ATTN_ROWS, HEAD_DIM), F32), pltpu.VMEM((ATTN_ROWS, HEAD_DIM), F32),
                            pltpu.VMEM((ATTN_ROWS, DA_VDIM), F32), pltpu.VMEM((ATTN_ROWS, page), F32)],
        ),
        out_shape=jax.ShapeDtypeStruct((nb, half, DA_VDIM), F32),
        compiler_params=_params("parallel", "arbitrary"),
    )(page_table, lam.reshape(1), map_rows(q), map_rows(k_new), v_rows, rbt, bucket_last,
      *([cache_k] * PAGES_PER_STEP), *([cache_v] * PAGES_PER_STEP), subln.reshape(1, DA_VDIM))
    return out[:, :DA_HEADS].reshape(nb, DA_WIDTH)


W_IN_SEGMENTS = (2 * GM_WIDTH, SB_WIDTH, SB_WIDTH, SB_WIDTH, DA_WIDTH, DA_WIDTH, DA_WIDTH)


def _lambda(wts, layer):
    lam_init = 0.8 - 0.6 * math.exp(-0.3 * layer)
    lam = (jnp.exp(jnp.sum(wts['da_lq1'][layer] * wts['da_lk1'][layer]))
           - jnp.exp(jnp.sum(wts['da_lq2'][layer] * wts['da_lk2'][layer])) + lam_init)
    return lam.astype(F32), lam_init


def _prompt_layer(x, mem_prompt_bf16, wts, layer, bsz, t, ff_pad, stacked):
    lam, lam_init = _lambda(wts, layer)
    depth = wts['w_in'].shape[0]
    h = rmsnorm(x, wts['norm_mix'][layer], BF16)
    cuts = [0]
    for width in W_IN_SEGMENTS:
        cuts.append(cuts[-1] + width)
    w_in = wts['w_in']
    a_in = matmul(h, w_in, layer, col0=cuts[0], ncols=W_IN_SEGMENTS[0])
    sb_q = matmul_heads(h, w_in, layer, col0=cuts[1], ncols=SB_WIDTH, slab=HEAD_DIM, bsz=bsz, t=t, out_dtype=BF16)
    heads_out = dict(bsz=bsz, t=t, depth=depth, out_layer=layer)
    sb_k = matmul_heads(h, w_in, layer, col0=cuts[2], ncols=SB_WIDTH, slab=HEAD_DIM, stacked=stacked[0], **heads_out)
    sb_v = matmul_heads(h, w_in, layer, col0=cuts[3], ncols=SB_WIDTH, slab=HEAD_DIM, stacked=stacked[1], **heads_out)
    da_q = matmul(h, w_in, layer, col0=cuts[4], ncols=DA_WIDTH, out_dtype=BF16)
    da_k, da_k_rows = matmul_token_rows(h, w_in, layer, col0=cuts[5], ncols=DA_WIDTH, stacked=stacked[3],
                                        **heads_out)
    da_v = matmul_heads(h, w_in, layer, col0=cuts[6], ncols=DA_WIDTH, slab=DA_VDIM, stacked=stacked[2], **heads_out)

    a_out, _ = gmlp(a_in, wts['gm_ln_g'][layer], wts['gm_ln_b'][layer], wts['gm_ws'][layer],
                    wts['gm_bs'][layer], wts['norm_a'][layer], single_token=False)
    b_out = rmsnorm(sb_attention(sb_q, sb_k, sb_v, layer, bsz, t), wts['norm_b'][layer], BF16)
    c_out = da_attention(da_q, da_k, da_v, layer, lam, wts['rel_bias'], wts['da_subln'][layer], bsz, t,
                         1.0 - lam_init)
    mix = jnp.concatenate([a_out, b_out, c_out], axis=-1)
    x = matmul(mix, wts['w_out'], layer, tn=1024, res=x)

    mem_len = mem_prompt_bf16.shape[0] // bsz
    mk = matmul(mem_prompt_bf16, wts['w_mk'], layer)
    mv = matmul(mem_prompt_bf16, wts['w_mv'], layer)
    hm = rmsnorm(x, wts['norm_mem'][layer], BF16)
    mq = matmul(hm, wts['w_mq'], layer, out_dtype=BF16)
    mo = mem_attention(mq, mk, mv, bsz, t, mem_len)
    x = matmul(mo, wts['w_mo'], layer, tn=1024, res=x)

    hf = rmsnorm(x, wts['norm_ffn'][layer], BF16)
    act, pconv = ffn_gate_up(hf, wts['w_gate'], wts['w_up'], wts['conv_w'], wts['conv_b'], layer, bsz, t)
    x = matmul(act, wts['w_down'], layer, nk=2, res=x)
    return x, (sb_k, sb_v, da_v, da_k_rows), (mk, mv, pconv[:, :, :pconv.shape[2] - ff_pad])


def _pad_rows(a, rows, cols=None):
    cols = a.shape[1] if cols is None else cols
    return jnp.pad(a, ((0, rows - a.shape[0]), (0, cols - a.shape[1])))


def _sample_layer(x, caches, mem_k, mem_v, conv_state, page_table, wts, layer, nb):
    lam, lam_init = _lambda(wts, layer)
    cache_sb_k, cache_sb_v, cache_da_k, cache_da_v = caches
    h = rmsnorm(x, wts['norm_mix'][layer], BF16)
    proj = matmul(h, wts['w_in'], layer, tn=1024)
    cuts = [0]
    for width in W_IN_SEGMENTS:
        cuts.append(cuts[-1] + width)
    a_in, sb_q, sb_k, sb_v, da_q, da_k, da_v = (proj[:, cuts[i]:cuts[i + 1]] for i in range(7))

    a_out, gv = gmlp(a_in, wts['gm_ln_g'][layer], wts['gm_ln_b'][layer], wts['gm_ws'][layer],
                     wts['gm_bs'][layer], wts['norm_a'][layer], single_token=True)
    b_out = sb_step_attention(sb_q[:nb], cache_sb_k, cache_sb_v, page_table, wts['norm_b'][layer], layer)
    c_out = da_step_attention(da_q[:nb], da_k[:nb], da_v[:nb], cache_da_k, cache_da_v, page_table, lam,
                              wts['rel_bias'], wts['da_subln'][layer], layer, 1.0 - lam_init)
    mix = jnp.concatenate([a_out, _pad_rows(b_out, SAMPLE_ROWS).astype(BF16),
                           _pad_rows(c_out, SAMPLE_ROWS).astype(BF16)], axis=-1)
    x = matmul(mix, wts['w_out'], layer, tn=1024, res=x)

    hm = rmsnorm(x, wts['norm_mem'][layer], BF16)
    mq = matmul(hm, wts['w_mq'], layer, out_dtype=BF16)
    mem_len = mem_k.shape[2]
    mq_rep = jnp.repeat(mq[:nb], SAMPLE_ROWS, axis=0)
    mo = mem_attention(mq_rep, mem_k[layer].reshape(nb * mem_len, MEM_WIDTH),
                       mem_v[layer].reshape(nb * mem_len, MEM_WIDTH), nb, SAMPLE_ROWS, mem_len)
    mo = _pad_rows(mo.reshape(nb, SAMPLE_ROWS, MEM_WIDTH)[:, 0], SAMPLE_ROWS)
    x = matmul(mo, wts['w_mo'], layer, tn=1024, res=x)

    hf = rmsnorm(x, wts['norm_ffn'][layer], BF16)
    prefix = conv_state[layer]
    d_ff, d_ffp = prefix.shape[2], wts['w_gate'].shape[2]
    act, g = ffn_gate_up_step(hf, wts['w_gate'], wts['w_up'], wts['conv_w'], wts['conv_b'],
                              _pad_rows(prefix[:, 0], SAMPLE_ROWS, d_ffp),
                              _pad_rows(prefix[:, 1], SAMPLE_ROWS, d_ffp), layer)
    x = matmul(act, wts['w_down'], layer, tn=1024, nk=2, res=x)
    sconv = jnp.stack([prefix[:, 1], g[:nb, :d_ff]], axis=1)
    return x, (sb_k[:nb], sb_v[:nb], da_k[:nb], da_v[:nb], gv[:nb], sconv)


def kernel(x_prompt, x_sample, cache_sb_k, cache_sb_v, cache_da_k, cache_da_v, cache_mem_k, cache_mem_v,
           state_conv, page_table, mem_prompt, norm_mix, w_in, gm_ln_g, gm_ln_b, gm_ws, gm_bs, norm_a, norm_b,
           da_lq1, da_lk1, da_lq2, da_lk2, da_subln, rel_bias, w_out, norm_mem, w_mq, w_mk, w_mv, w_mo,
           norm_ffn, w_gate, conv_w, conv_b, w_up, w_down, norm_final):
    bsz, t, d = x_prompt.shape
    nb = x_sample.shape[0]
    depth = w_in.shape[0]
    n_pool, page = cache_sb_k.shape[1], cache_sb_k.shape[2]
    mem_len = mem_prompt.shape[1]
    d_ff = w_gate.shape[2]
    ff_pad = -d_ff % FFN_TILE
    pad_last = lambda a: jnp.pad(a, ((0, 0),) * (a.ndim - 1) + ((0, ff_pad),))
    wts = {
        'norm_mix': norm_mix, 'w_in': w_in.astype(BF16), 'gm_ln_g': gm_ln_g, 'gm_ln_b': gm_ln_b, 'gm_ws': gm_ws,
        'gm_bs': gm_bs, 'norm_a': norm_a, 'norm_b': norm_b, 'da_lq1': da_lq1, 'da_lk1': da_lk1,
        'da_lq2': da_lq2, 'da_lk2': da_lk2, 'da_subln': da_subln, 'rel_bias': rel_bias,
        'w_out': w_out.astype(BF16), 'norm_mem': norm_mem, 'w_mq': w_mq.astype(BF16),
        'w_mk': w_mk.astype(BF16), 'w_mv': w_mv.astype(BF16), 'w_mo': w_mo.astype(BF16),
        'norm_ffn': norm_ffn, 'w_gate': pad_last(w_gate).astype(BF16), 'conv_w': pad_last(conv_w),
        'conv_b': pad_last(conv_b), 'w_up': pad_last(w_up).astype(BF16),
        'w_down': jnp.pad(w_down, ((0, 0), (0, ff_pad), (0, 0))).astype(BF16),
    }
    caches = (jnp.swapaxes(cache_sb_k, 2, 3), jnp.swapaxes(cache_sb_v, 2, 3),
              cache_da_k.reshape(depth, n_pool, page * DA_MAPS, HEAD_DIM), jnp.swapaxes(cache_da_v, 2, 3))
    mem_k = cache_mem_k.reshape(depth, nb, cache_mem_k.shape[2], MEM_WIDTH)
    mem_v = cache_mem_v.reshape(depth, nb, cache_mem_v.shape[2], MEM_WIDTH)
    mem_prompt_bf16 = mem_prompt.reshape(bsz * mem_len, d).astype(BF16)

    xp = x_prompt.reshape(bsz * t, d)
    xs = _pad_rows(x_sample.reshape(nb, d), SAMPLE_ROWS)
    p_new, s_new = [], []
    stacked = (None, None, None, None)
    for layer in range(depth):
        xp, stacked, p_state = _prompt_layer(xp, mem_prompt_bf16, wts, layer, bsz, t, ff_pad, stacked)
        p_new.append(p_state)
        xs, s_state = _sample_layer(xs, caches, mem_k, mem_v, state_conv, page_table, wts, layer, nb)
        s_new.append(s_state)
    g_final = norm_final
    y_prompt = rmsnorm(xp, g_final, F32).reshape(bsz, t, d)
    y_sample = rmsnorm(xs, g_final, F32)[:nb].reshape(nb, 1, d)

    def stack(states, idx, shape):
        return jnp.stack([s[idx] for s in states]).reshape((depth,) + shape)

    return (
        y_prompt, y_sample,
        jnp.swapaxes(stacked[0], 2, 3), jnp.swapaxes(stacked[1], 2, 3),
        stacked[3].reshape(depth, bsz, t, DA_HEADS, 2, HEAD_DIM), jnp.swapaxes(stacked[2], 2, 3),
        stack(p_new, 0, (bsz, mem_len, MEM_HEADS, HEAD_DIM)), stack(p_new, 1, (bsz, mem_len, MEM_HEADS, HEAD_DIM)),
        stack(p_new, 2, (bsz, CONV_W - 1, d_ff)),
        stack(s_new, 0, (nb, 1, SB_HEADS, HEAD_DIM)), stack(s_new, 1, (nb, 1, SB_HEADS, HEAD_DIM)),
        stack(s_new, 2, (nb, 1, DA_HEADS, 2, HEAD_DIM)), stack(s_new, 3, (nb, 1, DA_HEADS, DA_VDIM)),
        stack(s_new, 4, (nb, 1, GM_GROUPS, HEAD_DIM)), stack(s_new, 5, (nb, CONV_W - 1, d_ff)),
    )
```

```python
import functools
import math

import jax
import jax.numpy as jnp
from jax import lax
from jax.experimental import pallas as pl
from jax.experimental.pallas import tpu as pltpu

F32 = jnp.float32
BF16 = jnp.bfloat16

HEAD_DIM = 128
GM_GROUPS = 8
GM_WIDTH = GM_GROUPS * HEAD_DIM
CHUNK = 128
SB_HEADS = 12
SB_WIDTH = SB_HEADS * HEAD_DIM
DA_HEADS = 6
DA_VDIM = 2 * HEAD_DIM
DA_WIDTH = DA_HEADS * DA_VDIM
MEM_HEADS = 4
MEM_WIDTH = MEM_HEADS * HEAD_DIM
REL_BUCKETS = 32
REL_MAX_DIST = 128
CONV_W = 3
ATTN_SCALE = HEAD_DIM ** -0.5
EPS = 1e-6
NEG = -1e30

SAMPLE_ROWS = 16
ATTN_ROWS = 16
VMEM_LIMIT_BYTES = 56 * 1024 * 1024
FFN_TILE = 512
FFN_SUB = 256
SB_DECAY_LIMIT = 110.0
STRIP = 32
SB_GROUP = 4
DA_GROUP = 2


def _params(*sem):
    return pltpu.CompilerParams(dimension_semantics=sem, vmem_limit_bytes=VMEM_LIMIT_BYTES)


def _dot(a, b):
    return jnp.dot(a, b, preferred_element_type=F32)


def _dot_nt(a, b):
    return lax.dot_general(a, b, (((1,), (1,)), ((), ())), preferred_element_type=F32)


def _softplus(z):
    return jnp.maximum(z, 0.0) + jnp.log(1.0 + jnp.exp(-jnp.abs(z)))


def _split_bf16(x):
    hi = x.astype(BF16)
    lo = (x - hi.astype(F32)).astype(BF16)
    return hi, lo


def _by_strips(fn, n_out, rows, *arrays):
    outs = [[] for _ in range(n_out)]
    for r0 in range(0, rows, STRIP):
        res = fn(r0, *[a[r0:r0 + STRIP] for a in arrays])
        for acc, r in zip(outs, res):
            acc.append(r)
    return [jnp.concatenate(o, axis=0) for o in outs]


def _rmsnorm_kernel(x_ref, g_ref, o_ref):
    x = x_ref[...].astype(F32)
    y = x * lax.rsqrt(jnp.mean(x * x, axis=-1, keepdims=True) + EPS)
    o_ref[...] = (y * g_ref[...]).astype(o_ref.dtype)


def rmsnorm(x, g, out_dtype, tm=256):
    m, d = x.shape
    tm = min(tm, m)
    return pl.pallas_call(
        _rmsnorm_kernel,
        grid=(m // tm,),
        in_specs=[pl.BlockSpec((tm, d), lambda i: (i, 0)),
                  pl.BlockSpec((1, d), lambda i: (0, 0))],
        out_specs=pl.BlockSpec((tm, d), lambda i: (i, 0)),
        out_shape=jax.ShapeDtypeStruct((m, d), out_dtype),
        compiler_params=_params("parallel"),
    )(x, g.reshape(1, d).astype(F32))


def _mm_kernel(*refs, nk, has_res):
    x_ref, w_ref = refs[0], refs[1]
    r_ref = refs[2] if has_res else None
    o_ref = refs[2 + has_res]
    part = _dot(x_ref[...], w_ref[...])
    if nk == 1:
        if has_res:
            part = part + r_ref[...]
        o_ref[...] = part.astype(o_ref.dtype)
        return
    acc_ref = refs[3 + has_res]
    k = pl.program_id(2)

    @pl.when(k == 0)
    def _():
        acc_ref[...] = part

    @pl.when(k > 0)
    def _():
        acc_ref[...] += part

    @pl.when(k == nk - 1)
    def _():
        r = acc_ref[...]
        if has_res:
            r = r + r_ref[...]
        o_ref[...] = r.astype(o_ref.dtype)


def matmul(x, w, layer, *, col0=0, ncols=None, tm=1024, tn=512, nk=1, res=None, out_dtype=F32):
    m, kdim = x.shape
    n_total = w.shape[2]
    ncols = n_total if ncols is None else ncols
    tm = min(tm, m)
    tn = min(tn, ncols)
    while ncols % tn or col0 % tn:
        tn //= 2
    tk = kdim // nk
    jb = col0 // tn
    in_specs = [pl.BlockSpec((tm, tk), lambda i, j, k: (i, k)),
                pl.BlockSpec((None, tk, tn), lambda i, j, k: (layer, k, jb + j))]
    args = [x, w]
    if res is not None:
        in_specs.append(pl.BlockSpec((tm, tn), lambda i, j, k: (i, j)))
        args.append(res)
    scratch = [pltpu.VMEM((tm, tn), F32)] if nk > 1 else []
    return pl.pallas_call(
        functools.partial(_mm_kernel, nk=nk, has_res=res is not None),
        grid=(m // tm, ncols // tn, nk),
        in_specs=in_specs,
        out_specs=pl.BlockSpec((tm, tn), lambda i, j, k: (i, j)),
        out_shape=jax.ShapeDtypeStruct((m, ncols), out_dtype),
        scratch_shapes=scratch,
        compiler_params=_params("parallel", "parallel", "arbitrary"),
    )(*args)


def _mm_parts_kernel(*refs, n_x):
    x_refs = refs[:n_x]
    w_ref, r_ref, o_ref = refs[n_x:]
    acc = r_ref[...]
    k0 = 0
    for x_ref in x_refs:
        k = x_ref.shape[1]
        acc = acc + _dot(x_ref[...], w_ref[k0:k0 + k, :])
        k0 += k
    o_ref[...] = acc.astype(o_ref.dtype)


def matmul_parts(xs, w, layer, res, *, tm=1024, tn=1024):
    m = xs[0].shape[0]
    kdim, n = w.shape[1], w.shape[2]
    tm = min(tm, m)
    tn = min(tn, n)
    in_specs = [pl.BlockSpec((tm, x.shape[1]), lambda i, j: (i, 0)) for x in xs]
    in_specs += [pl.BlockSpec((None, kdim, tn), lambda i, j: (layer, 0, j)),
                 pl.BlockSpec((tm, tn), lambda i, j: (i, j))]
    return pl.pallas_call(
        functools.partial(_mm_parts_kernel, n_x=len(xs)),
        grid=(m // tm, n // tn),
        in_specs=in_specs,
        out_specs=pl.BlockSpec((tm, tn), lambda i, j: (i, j)),
        out_shape=jax.ShapeDtypeStruct((m, n), F32),
        compiler_params=_params("parallel", "parallel"),
    )(*xs, w, res)


def _mm_heads_kernel(*refs, slab):
    x_ref, w_ref, o_ref = refs[0], refs[1], refs[-1]
    part = _dot(x_ref[...], w_ref[...])
    for s in range(o_ref.shape[0]):
        o_ref[s] = part[:, s * slab:(s + 1) * slab].astype(o_ref.dtype)


def matmul_heads(x, w, layer, *, col0, ncols, slab, bsz, t, depth=1, out_layer=0, stacked=None,
                 tm=1024, tn=512, out_dtype=F32):
    m, kdim = x.shape
    tm = min(tm, t)
    tn = min(tn, ncols)
    jb = col0 // tn
    tiles_per_seq = t // tm
    in_specs = [pl.BlockSpec((tm, kdim), lambda i, j: (i, 0)),
                pl.BlockSpec((None, kdim, tn), lambda i, j: (layer, 0, jb + j))]
    args = [x, w]
    aliases = {}
    if stacked is not None:
        in_specs.append(pl.BlockSpec(memory_space=pl.ANY))
        args.append(stacked)
        aliases = {2: 0}
    return pl.pallas_call(
        functools.partial(_mm_heads_kernel, slab=slab),
        grid=(m // tm, ncols // tn),
        in_specs=in_specs,
        out_specs=pl.BlockSpec((None, None, tn // slab, tm, slab),
                               lambda i, j: (out_layer, i // tiles_per_seq, j, i % tiles_per_seq, 0)),
        out_shape=jax.ShapeDtypeStruct((depth, bsz, ncols // slab, t, slab), out_dtype),
        input_output_aliases=aliases,
        compiler_params=_params("parallel", "parallel"),
    )(*args)


def _mm_token_rows_kernel(*refs, n_w):
    x_ref, w_refs = refs[0], refs[1:1 + n_w]
    std_ref, rows_ref = refs[-2], refs[-1]
    tm = x_ref.shape[0]
    tn = w_refs[0].shape[1]
    per_w = tn // HEAD_DIM
    n_blocks = n_w * per_w
    x = x_ref[...]
    for s in range(n_w):
        part = _dot(x, w_refs[s][...])
        std_ref[:, s * tn:(s + 1) * tn] = part.astype(std_ref.dtype)
        for c in range(per_w):
            rows_ref[pl.ds(s * per_w + c, tm, stride=n_blocks), :] = part[:, c * HEAD_DIM:(c + 1) * HEAD_DIM]


def matmul_token_rows(x, w, layer, *, col0, ncols, bsz, t, depth, out_layer, stacked, tm=512, tn=512):
    m, kdim = x.shape
    tm = min(tm, t)
    n_w = ncols // tn
    n_blocks = ncols // HEAD_DIM
    tiles_per_seq = t // tm
    in_specs = [pl.BlockSpec((tm, kdim), lambda i: (i, 0))]
    in_specs += [pl.BlockSpec((None, kdim, tn), lambda i, s=s: (layer, 0, col0 // tn + s)) for s in range(n_w)]
    args = [x] + [w] * n_w
    aliases = {}
    if stacked is not None:
        in_specs.append(pl.BlockSpec(memory_space=pl.ANY))
        args.append(stacked)
        aliases = {1 + n_w: 1}
    return pl.pallas_call(
        functools.partial(_mm_token_rows_kernel, n_w=n_w),
        grid=(m // tm,),
        in_specs=in_specs,
        out_specs=[pl.BlockSpec((tm, ncols), lambda i: (i, 0)),
                   pl.BlockSpec((None, None, tm * n_blocks, HEAD_DIM),
                                lambda i: (out_layer, i // tiles_per_seq, i % tiles_per_seq, 0))],
        out_shape=[jax.ShapeDtypeStruct((m, ncols), BF16),
                   jax.ShapeDtypeStruct((depth, bsz, t * n_blocks, HEAD_DIM), F32)],
        input_output_aliases=aliases,
        compiler_params=_params("parallel"),
    )(*args)


def _gmlp_kernel(a_ref, lng_ref, lnb_ref, ws_ref, bs_ref, na_ref, o_ref, gv_ref, *, single_token):
    uv = jax.nn.gelu(a_ref[...])
    if not single_token:
        r = lax.broadcasted_iota(jnp.int32, (CHUNK, CHUNK), 0)
        c = lax.broadcasted_iota(jnp.int32, (CHUNK, CHUNK), 1)
        tril = r >= c
    outs = []
    for g in range(GM_GROUPS):
        lo, hi = g * HEAD_DIM, (g + 1) * HEAD_DIM
        u = uv[:, lo:hi]
        v = uv[:, GM_WIDTH + lo:GM_WIDTH + hi]
        vc = v - jnp.mean(v, axis=-1, keepdims=True)
        y = vc * lax.rsqrt(jnp.mean(vc * vc, axis=-1, keepdims=True) + EPS)
        gv = y * lng_ref[:, lo:hi] + lnb_ref[:, lo:hi]
        gv_ref[:, lo:hi] = gv
        if single_token:
            mix = ws_ref[:, lo:hi] * gv + bs_ref[:, lo:hi]
        else:
            wm = jnp.where(tril, ws_ref[g], 0.0).astype(BF16)
            mix = _dot(wm, gv.astype(BF16)) + bs_ref[:, g:g + 1]
        outs.append(u * mix)
    t = jnp.concatenate(outs, axis=-1)
    y = t * lax.rsqrt(jnp.mean(t * t, axis=-1, keepdims=True) + EPS)
    o_ref[...] = (y * na_ref[...]).astype(o_ref.dtype)


def gmlp(a_in, ln_g, ln_b, ws, bs, norm_a, *, single_token):
    m = a_in.shape[0]
    rows = m if single_token else CHUNK
    full = lambda shape: pl.BlockSpec(shape, lambda i: (0,) * len(shape))
    if single_token:
        ws_arg = jnp.repeat(ws[:, 0, 0], HEAD_DIM).reshape(1, GM_WIDTH)
        bs_arg = jnp.repeat(bs[:, 0], HEAD_DIM).reshape(1, GM_WIDTH)
    else:
        ws_arg = ws
        bs_arg = bs.T
    return pl.pallas_call(
        functools.partial(_gmlp_kernel, single_token=single_token),
        grid=(m // rows,),
        in_specs=[pl.BlockSpec((rows, 2 * GM_WIDTH), lambda i: (i, 0)),
                  full((1, GM_WIDTH)), full((1, GM_WIDTH)),
                  full(ws_arg.shape), full(bs_arg.shape), full((1, GM_WIDTH))],
        out_specs=[pl.BlockSpec((rows, GM_WIDTH), lambda i: (i, 0)),
                   pl.BlockSpec((rows, GM_WIDTH), lambda i: (i, 0))],
        out_shape=[jax.ShapeDtypeStruct((m, GM_WIDTH), BF16),
                   jax.ShapeDtypeStruct((m, GM_WIDTH), F32)],
        compiler_params=_params("parallel"),
    )(a_in, ln_g.reshape(1, GM_WIDTH), ln_b.reshape(1, GM_WIDTH), ws_arg, bs_arg,
      norm_a.reshape(1, GM_WIDTH))


def _sb_kernel(q_ref, k_ref, v_ref, o_ref, kb_ref, vb_ref, *, t, tq):
    group = q_ref.shape[0]
    heads = range(group)
    kb_ref[...] = k_ref[...].astype(BF16)
    vb_ref[...] = v_ref[...].astype(BF16)
    r = lax.broadcasted_iota(jnp.int32, (tq, tq), 0)
    c = lax.broadcasted_iota(jnp.int32, (tq, tq), 1)
    later = jnp.where(r > c, 1.0, 0.0).astype(BF16)
    later2 = jnp.concatenate([later, later], axis=0)
    srow = lax.broadcasted_iota(jnp.int32, (STRIP, tq), 0)
    scol = lax.broadcasted_iota(jnp.int32, (STRIP, tq), 1)

    def tile(qb, kj, state, masked):
        start = pl.multiple_of(kj * tq, tq)
        z = [_dot_nt(qb[g], kb_ref[g, pl.ds(start, tq), :]) for g in heads]

        def front(r0, zs):
            t = zs * ATTN_SCALE
            p = _softplus(t)
            tmp = t - p
            if masked:
                p = jnp.where(scol < srow + r0, p, 0.0)
            hi, lo = _split_bf16(p)
            return jnp.concatenate([hi, lo], axis=1), tmp, p[:, 0:1]

        fr = [_by_strips(front, 3, tq, z[g]) for g in heads]
        cum = [_dot(fr[g][0], later2) for g in heads]

        def back(r0, tmp, cm, run):
            a = jnp.exp(tmp - (run + cm))
            if masked:
                a = jnp.where(scol < srow + r0, a, 0.0)
            return (a.astype(BF16),)

        a = [_by_strips(back, 1, tq, fr[g][1], cum[g], state[g][0])[0] for g in heads]
        pv = [_dot(a[g], vb_ref[g, pl.ds(start, tq), :]) for g in heads]
        return tuple((state[g][0] + cum[g][:, 0:1] + fr[g][2], state[g][1] + pv[g]) for g in heads)

    def q_body(qi, carry):
        qstart = pl.multiple_of(qi * tq, tq)
        qb = [q_ref[g, pl.ds(qstart, tq), :] for g in heads]
        init = tuple((jnp.zeros((tq, 1), F32), jnp.zeros((tq, HEAD_DIM), F32)) for _ in heads)
        state = tile(qb, qi, init, True)

        def cond(carry):
            j, st = carry
            low = st[0][0]
            for g in heads[1:]:
                low = jnp.minimum(low, st[g][0])
            return jnp.logical_and(j < qi, jnp.min(low) < SB_DECAY_LIMIT)

        _, state = lax.while_loop(cond, lambda cr: (cr[0] + 1, tile(qb, qi - 1 - cr[0], cr[1], False)),
                                  (jnp.int32(0), state))
        o_ref[pl.ds(qstart, tq), :] = jnp.concatenate([acc for _, acc in state], axis=1)
        return carry

    lax.fori_loop(0, t // tq, q_body, 0)


def _attn_tq(t):
    return 256 if t % 256 == 0 and t >= 1024 else 128


def sb_attention(q, k, v, layer, bsz, t):
    tq = _attn_tq(t)
    group = SB_GROUP

    def spec(lead):
        return pl.BlockSpec((None, None, group, t, HEAD_DIM), lambda b, h: (lead, b, h, 0, 0))

    return pl.pallas_call(
        functools.partial(_sb_kernel, t=t, tq=tq),
        grid=(bsz, SB_HEADS // group),
        in_specs=[spec(0), spec(layer), spec(layer)],
        out_specs=pl.BlockSpec((t, group * HEAD_DIM), lambda b, h: (b, h)),
        out_shape=jax.ShapeDtypeStruct((bsz * t, SB_WIDTH), F32),
        scratch_shapes=[pltpu.VMEM((group, t, HEAD_DIM), BF16), pltpu.VMEM((group, t, HEAD_DIM), BF16)],
        compiler_params=_params("parallel", "parallel"),
    )(q, k, v)


def rel_bucket_of_distance(n):
    max_exact = REL_BUCKETS // 2
    nf = jnp.maximum(n, 1).astype(F32)
    large = max_exact + (jnp.log(nf / max_exact) / math.log(REL_MAX_DIST / max_exact)
                         * (REL_BUCKETS - max_exact)).astype(jnp.int32)
    large = jnp.minimum(large, REL_BUCKETS - 1)
    return jnp.where(n < max_exact, n, large)


def _bias_from_buckets(buckets, rb_ref, head):
    bias = jnp.zeros(buckets.shape, F32)
    for b in range(REL_BUCKETS):
        bias = jnp.where(buckets == b, rb_ref[b, head], bias)
    return bias


def _da_kernel(lam_ref, rb_ref, q_ref, k_ref, v_ref, bd_ref, bo_ref, g_ref, o_ref,
               kb_ref, vb_ref, biasd_ref, biaso_ref, *, t, tq, out_scale):
    group = v_ref.shape[0]
    head0 = pl.program_id(1) * group
    lam = lam_ref[0]
    chains = [(g, mi) for g in range(group) for mi in range(2)]
    cols = [g * DA_VDIM + mi * HEAD_DIM for g, mi in chains]
    kb_ref[...] = k_ref[...].astype(BF16)
    vb_ref[...] = v_ref[...].astype(BF16)
    for g in range(group):
        biasd_ref[g] = _bias_from_buckets(bd_ref[...], rb_ref, head0 + g)
        biaso_ref[g] = _bias_from_buckets(bo_ref[...], rb_ref, head0 + g)
    bias_far = [rb_ref[REL_BUCKETS - 1, head0 + g] for g in range(group)]
    r = lax.broadcasted_iota(jnp.int32, (tq, tq), 0)
    c = lax.broadcasted_iota(jnp.int32, (tq, tq), 1)
    causal = c <= r

    def tile(qb, kj, state, bias, masked):
        start = pl.multiple_of(kj * tq, tq)
        kblk = kb_ref[pl.ds(start, tq), :]
        n = range(len(chains))
        s = [_dot_nt(qb[:, cols[i]:cols[i] + HEAD_DIM], kblk[:, cols[i]:cols[i] + HEAD_DIM]) * ATTN_SCALE
             + bias[chains[i][0]] for i in n]
        if masked:
            s = [jnp.where(causal, s[i], NEG) for i in n]
        m_new = [jnp.maximum(state[i][0], jnp.max(s[i], axis=-1, keepdims=True)) for i in n]
        alpha = [jnp.exp(state[i][0] - m_new[i]) for i in n]
        p = [jnp.exp(s[i] - m_new[i]) for i in n]
        l_new = [alpha[i] * state[i][1] + jnp.sum(p[i], axis=-1, keepdims=True) for i in n]
        pv = [_dot(p[i].astype(BF16), vb_ref[chains[i][0], pl.ds(start, tq), :]) for i in n]
        return tuple((m_new[i], l_new[i], alpha[i] * state[i][2] + pv[i]) for i in n)

    def q_body(qi, carry):
        qstart = pl.multiple_of(qi * tq, tq)
        qb = q_ref[pl.ds(qstart, tq), :]
        init = tuple((jnp.full((tq, 1), NEG, F32), jnp.zeros((tq, 1), F32), jnp.zeros((tq, DA_VDIM), F32))
                     for _ in chains)
        state = tile(qb, qi, init, [biasd_ref[g] for g in range(group)], True)
        state = lax.cond(qi > 0,
                         lambda st: tile(qb, qi - 1, st, [biaso_ref[g] for g in range(group)], False),
                         lambda st: st, state)
        state = lax.fori_loop(1, qi, lambda j, st: tile(qb, qi - 1 - j, st, bias_far, False), state)
        outs = []
        for g in range(group):
            (_, l1, a1), (_, l2, a2) = state[2 * g], state[2 * g + 1]
            cvec = a1 / l1 - lam * (a2 / l2)
            y = cvec * lax.rsqrt(jnp.mean(cvec * cvec, axis=-1, keepdims=True) + EPS)
            outs.append(y * g_ref[...] * out_scale)
        o_ref[pl.ds(qstart, tq), :] = jnp.concatenate(outs, axis=1).astype(o_ref.dtype)
        return carry

    lax.fori_loop(0, t // tq, q_body, 0)


def da_attention(q, k, v, layer, lam, rel_bias, subln, bsz, t, out_scale):
    tq = _attn_tq(t)
    group = DA_GROUP
    pos = jnp.arange(tq, dtype=jnp.int32)
    dist = pos[:, None] - pos[None, :]
    bucket_diag = rel_bucket_of_distance(jnp.maximum(dist, 0))
    bucket_off = rel_bucket_of_distance(dist + tq)
    spec = pl.BlockSpec((t, group * DA_VDIM), lambda b, h: (b, h))
    smem = pl.BlockSpec(memory_space=pltpu.SMEM)
    const = lambda shape: pl.BlockSpec(shape, lambda b, h: (0,) * len(shape))
    return pl.pallas_call(
        functools.partial(_da_kernel, t=t, tq=tq, out_scale=out_scale),
        grid=(bsz, DA_HEADS // group),
        in_specs=[smem, smem, spec, spec,
                  pl.BlockSpec((None, None, group, t, DA_VDIM), lambda b, h: (layer, b, h, 0, 0)),
                  const((tq, tq)), const((tq, tq)), const((1, DA_VDIM))],
        out_specs=spec,
        out_shape=jax.ShapeDtypeStruct((bsz * t, DA_WIDTH), BF16),
        scratch_shapes=[pltpu.VMEM((t, group * DA_VDIM), BF16), pltpu.VMEM((group, t, DA_VDIM), BF16),
                        pltpu.VMEM((group, tq, tq), F32), pltpu.VMEM((group, tq, tq), F32)],
        compiler_params=_params("parallel", "parallel"),
    )(lam.reshape(1), rel_bias, q, k, v, bucket_diag, bucket_off, subln.reshape(1, DA_VDIM))


def _mem_kernel(q_ref, k_ref, v_ref, o_ref):
    outs = []
    for h in range(MEM_HEADS):
        lo, hi = h * HEAD_DIM, (h + 1) * HEAD_DIM
        s = _dot_nt(q_ref[:, lo:hi], k_ref[:, lo:hi].astype(BF16)) * ATTN_SCALE
        p = jnp.exp(s - jnp.max(s, axis=-1, keepdims=True))
        den = jnp.sum(p, axis=-1, keepdims=True)
        outs.append(_dot(p.astype(BF16), v_ref[:, lo:hi].astype(BF16)) / den)
    o_ref[...] = jnp.concatenate(outs, axis=-1).astype(o_ref.dtype)


def mem_attention(q, mk, mv, bsz, t, mem_len, tq=512):
    tq = min(tq, t)
    nq = t // tq
    kv_spec = pl.BlockSpec((mem_len, MEM_WIDTH), lambda b, i: (b, 0))
    return pl.pallas_call(
        _mem_kernel,
        grid=(bsz, nq),
        in_specs=[pl.BlockSpec((tq, MEM_WIDTH), lambda b, i: (b * nq + i, 0)), kv_spec, kv_spec],
        out_specs=pl.BlockSpec((tq, MEM_WIDTH), lambda b, i: (b * nq + i, 0)),
        out_shape=jax.ShapeDtypeStruct((bsz * t, MEM_WIDTH), BF16),
        compiler_params=_params("parallel", "parallel"),
    )(q, mk, mv)


def _mem_block_kernel(x_ref, gm_ref, wq_ref, k_ref, v_ref, wo_ref, gf_ref, xo_ref, hf_ref):
    x = x_ref[...]
    hm = (x * lax.rsqrt(jnp.mean(x * x, axis=-1, keepdims=True) + EPS) * gm_ref[...]).astype(BF16)
    mq = _dot(hm, wq_ref[...]).astype(BF16)
    outs = []
    for h in range(MEM_HEADS):
        lo, hi = h * HEAD_DIM, (h + 1) * HEAD_DIM
        s = _dot_nt(mq[:, lo:hi], k_ref[:, lo:hi].astype(BF16)) * ATTN_SCALE
        p = jnp.exp(s - jnp.max(s, axis=-1, keepdims=True))
        den = jnp.sum(p, axis=-1, keepdims=True)
        outs.append(_dot(p.astype(BF16), v_ref[:, lo:hi].astype(BF16)) / den)
    mo = jnp.concatenate(outs, axis=-1).astype(BF16)
    x_new = x + _dot(mo, wo_ref[...])
    xo_ref[...] = x_new
    y = x_new * lax.rsqrt(jnp.mean(x_new * x_new, axis=-1, keepdims=True) + EPS)
    hf_ref[...] = (y * gf_ref[...]).astype(hf_ref.dtype)


def mem_block(x, mk, mv, norm_mem, w_mq, w_mo, norm_ffn, layer, bsz, t, mem_len, tm=256):
    m, d = x.shape
    tm = min(tm, t)
    nq = t // tm
    rows = pl.BlockSpec((tm, d), lambda b, i: (b * nq + i, 0))
    gain = pl.BlockSpec((1, d), lambda b, i: (0, 0))
    kv = pl.BlockSpec((mem_len, MEM_WIDTH), lambda b, i: (b, 0))
    return pl.pallas_call(
        _mem_block_kernel,
        grid=(bsz, nq),
        in_specs=[rows, gain, pl.BlockSpec((None, d, MEM_WIDTH), lambda b, i: (layer, 0, 0)), kv, kv,
                  pl.BlockSpec((None, MEM_WIDTH, d), lambda b, i: (layer, 0, 0)), gain],
        out_specs=[rows, rows],
        out_shape=[jax.ShapeDtypeStruct((m, d), F32), jax.ShapeDtypeStruct((m, d), BF16)],
        compiler_params=_params("parallel", "parallel"),
    )(x, norm_mem.reshape(1, d), w_mq, mk, mv, w_mo, norm_ffn.reshape(1, d))


def _silu(x):
    return x / (1.0 + jnp.exp(-x))


def _ffn_kernel(h_ref, wg_ref, wu_ref, cw_ref, cb_ref, *rest, tm, tiles_per_seq):
    act_ref, pc_ref, gbuf_ref = rest[-3:]
    i = pl.program_id(1)
    first = (i % tiles_per_seq) == 0
    h = h_ref[...]
    for c in range(gbuf_ref.shape[0]):
        cols = slice(c * FFN_SUB, (c + 1) * FFN_SUB)
        buf = gbuf_ref.at[c]
        g = _dot(h, wg_ref[:, cols])
        u = _dot(h, wu_ref[:, cols])
        buf[0:8, :] = jnp.where(first, 0.0, buf[tm:tm + 8, :])
        buf[8:tm + 8, :] = g
        g1 = buf[7:tm + 7, :]
        g2 = buf[6:tm + 6, :]
        gc = cb_ref[:, cols] + cw_ref[0:1, cols] * g2 + cw_ref[1:2, cols] * g1 + cw_ref[2:3, cols] * g
        act_ref[:, cols] = (_silu(gc) * u).astype(act_ref.dtype)
        pc_ref[:, cols] = g[tm - (CONV_W - 1):tm, :]


def _ffn_gate_up_cols(h, wg, wu, conv_w, conv_b, layer, bsz, t, *, col0, ncols, tn, bufs, tm=1024):
    m, d = h.shape
    d_ff = wg.shape[2]
    tm = min(tm, t)
    jb = col0 // tn
    tiles_per_seq = t // tm
    wspec = pl.BlockSpec((None, d, tn), lambda j, i: (layer, 0, jb + j))
    in_specs = [pl.BlockSpec((tm, d), lambda j, i: (i, 0)), wspec, wspec,
                pl.BlockSpec((None, CONV_W, tn), lambda j, i: (layer, 0, jb + j)),
                pl.BlockSpec((None, 1, tn), lambda j, i: (layer, 0, jb + j))]
    args = [h, wg, wu, conv_w, conv_b.reshape(conv_b.shape[0], 1, d_ff)]
    aliases = {}
    if bufs is not None:
        in_specs += [pl.BlockSpec(memory_space=pl.ANY)] * 2
        args += list(bufs)
        aliases = {5: 0, 6: 1}
    return pl.pallas_call(
        functools.partial(_ffn_kernel, tm=tm, tiles_per_seq=tiles_per_seq),
        grid=(ncols // tn, m // tm),
        in_specs=in_specs,
        out_specs=[pl.BlockSpec((tm, tn), lambda j, i: (i, jb + j)),
                   pl.BlockSpec((None, CONV_W - 1, tn), lambda j, i: (i // tiles_per_seq, 0, jb + j))],
        out_shape=[jax.ShapeDtypeStruct((m, d_ff), BF16),
                   jax.ShapeDtypeStruct((bsz, CONV_W - 1, d_ff), F32)],
        scratch_shapes=[pltpu.VMEM((tn // FFN_SUB, tm + 8, FFN_SUB), F32)],
        input_output_aliases=aliases,
        compiler_params=_params("parallel", "arbitrary"),
    )(*args)


def ffn_gate_up(h, wg, wu, conv_w, conv_b, layer, bsz, t):
    d_ff = wg.shape[2]
    main = d_ff // FFN_TILE * FFN_TILE
    bufs = None
    if main:
        bufs = _ffn_gate_up_cols(h, wg, wu, conv_w, conv_b, layer, bsz, t, col0=0, ncols=main, tn=FFN_TILE,
                                 bufs=None)
    if d_ff > main:
        bufs = _ffn_gate_up_cols(h, wg, wu, conv_w, conv_b, layer, bsz, t, col0=main, ncols=d_ff - main,
                                 tn=d_ff - main, bufs=bufs)
    return bufs


def _ffn_step_kernel(h_ref, wg_ref, wu_ref, cw_ref, cb_ref, p0_ref, p1_ref, act_ref, g_ref):
    h = h_ref[...]
    g = _dot(h, wg_ref[...])
    u = _dot(h, wu_ref[...])
    gc = cb_ref[...] + cw_ref[0:1, :] * p0_ref[...] + cw_ref[1:2, :] * p1_ref[...] + cw_ref[2:3, :] * g
    act_ref[...] = (_silu(gc) * u).astype(act_ref.dtype)
    g_ref[...] = g


def ffn_gate_up_step(h, wg, wu, conv_w, conv_b, prefix0, prefix1, layer, *, tn=512):
    m, d = h.shape
    d_ff = wg.shape[2]
    while d_ff % tn:
        tn //= 2
    wspec = pl.BlockSpec((None, d, tn), lambda j: (layer, 0, j))
    row = pl.BlockSpec((m, tn), lambda j: (0, j))
    return pl.pallas_call(
        _ffn_step_kernel,
        grid=(d_ff // tn,),
        in_specs=[pl.BlockSpec((m, d), lambda j: (0, 0)), wspec, wspec,
                  pl.BlockSpec((None, CONV_W, tn), lambda j: (layer, 0, j)),
                  pl.BlockSpec((None, 1, tn), lambda j: (layer, 0, j)), row, row],
        out_specs=[row, row],
        out_shape=[jax.ShapeDtypeStruct((m, d_ff), BF16), jax.ShapeDtypeStruct((m, d_ff), F32)],
        compiler_params=_params("parallel"),
    )(h, wg, wu, conv_w, conv_b.reshape(conv_b.shape[0], 1, d_ff), prefix0, prefix1)


PAGES_PER_STEP = 8
DA_MAPS = 2 * DA_HEADS


def _page_specs(layer, n_pages, block):
    specs = []
    for s in range(PAGES_PER_STEP):
        def imap(b, p, pt, s=s):
            return (layer, pt[b, n_pages - 1 - (p * PAGES_PER_STEP + s)]) + (0,) * len(block)
        specs.append(pl.BlockSpec((None, None) + block, imap))
    return specs


def _sb_step_kernel(pt_ref, q_ref, kc_ref, vc_ref, g_ref, o_ref, kbuf_ref, vbuf_ref, sem_ref, *, layer, n_pages):
    b = pl.program_id(0)
    page = kbuf_ref.shape[2]
    rowid = lax.broadcasted_iota(jnp.int32, (ATTN_ROWS, HEAD_DIM), 0)
    q = q_ref[...]
    q_rows = [jnp.where(rowid == h, q, 0.0).astype(BF16) for h in range(SB_HEADS)]
    r = lax.broadcasted_iota(jnp.int32, (page, page), 0)
    c = lax.broadcasted_iota(jnp.int32, (page, page), 1)
    later = jnp.where(r > c, 1.0, 0.0).astype(BF16)
    later2 = jnp.concatenate([later, later], axis=0)
    half = SB_HEADS // 2

    def page_copies(i, slot):
        pg = pt_ref[b, n_pages - 1 - i]
        return (pltpu.make_async_copy(kc_ref.at[layer, pg], kbuf_ref.at[slot], sem_ref.at[0, slot]),
                pltpu.make_async_copy(vc_ref.at[layer, pg], vbuf_ref.at[slot], sem_ref.at[1, slot]))

    def not_decayed(run):
        return jnp.min(jnp.where(rowid[:, 0:1] < SB_HEADS, run, SB_DECAY_LIMIT)) < SB_DECAY_LIMIT

    def cond(carry):
        i, run, _ = carry
        return jnp.logical_and(i < n_pages, not_decayed(run))

    def body(carry):
        i, run, acc = carry
        slot = i % 2
        for cp in page_copies(i, slot):
            cp.wait()

        @pl.when(i + 1 < n_pages)
        def _():
            for cp in page_copies(i + 1, 1 - slot):
                cp.start()

        parts = []
        for h0 in (0, half):
            zp = _dot_nt(q_rows[h0], kbuf_ref[slot, h0].astype(BF16))
            for h in range(h0 + 1, h0 + half):
                zp = zp + _dot_nt(q_rows[h], kbuf_ref[slot, h].astype(BF16))
            parts.append(zp)
        t = (parts[0] + parts[1]) * ATTN_SCALE
        p = _softplus(t)
        hi, lo = _split_bf16(p)
        cum = _dot(jnp.concatenate([hi, lo], axis=1), later2)
        a = jnp.exp((t - p) - (run + cum)).astype(BF16)
        for h in range(SB_HEADS):
            acc = acc + jnp.where(rowid == h, _dot(a, vbuf_ref[slot, h].astype(BF16)), 0.0)
        return i + 1, run + jnp.sum(p, axis=-1, keepdims=True), acc

    for cp in page_copies(0, 0):
        cp.start()
    init = (jnp.int32(0), jnp.zeros((ATTN_ROWS, 1), F32), jnp.zeros((ATTN_ROWS, HEAD_DIM), F32))
    done, _, acc = lax.while_loop(cond, body, init)

    @pl.when(done < n_pages)
    def _():
        for cp in page_copies(done, done % 2):
            cp.wait()

    ms = jnp.sum(jnp.sum(acc * acc, axis=-1, keepdims=True), axis=0, keepdims=True) / SB_WIDTH
    o_ref[...] = acc * lax.rsqrt(ms + EPS) * g_ref[...]


def sb_step_attention(q, cache_k, cache_v, page_table, norm_b, layer):
    nb, n_pages = page_table.shape
    page = cache_k.shape[3]
    pad_heads = ((0, 0), (0, ATTN_ROWS - SB_HEADS), (0, 0))
    q_rows = jnp.pad(q.reshape(nb, SB_HEADS, HEAD_DIM), pad_heads)
    gain = jnp.pad(norm_b.reshape(SB_HEADS, HEAD_DIM), pad_heads[1:])
    row = pl.BlockSpec((None, ATTN_ROWS, HEAD_DIM), lambda b, pt: (b, 0, 0))
    hbm = pl.BlockSpec(memory_space=pl.ANY)
    out = pl.pallas_call(
        functools.partial(_sb_step_kernel, layer=layer, n_pages=n_pages),
        grid_spec=pltpu.PrefetchScalarGridSpec(
            num_scalar_prefetch=1,
            grid=(nb,),
            in_specs=[row, hbm, hbm, pl.BlockSpec((ATTN_ROWS, HEAD_DIM), lambda b, pt: (0, 0))],
            out_specs=row,
            scratch_shapes=[pltpu.VMEM((2, SB_HEADS, page, HEAD_DIM), F32),
                            pltpu.VMEM((2, SB_HEADS, page, HEAD_DIM), F32),
                            pltpu.SemaphoreType.DMA((2, 2))],
        ),
        out_shape=jax.ShapeDtypeStruct((nb, ATTN_ROWS, HEAD_DIM), F32),
        compiler_params=_params("arbitrary"),
    )(page_table, q_rows, cache_k, cache_v, gain)
    return out[:, :SB_HEADS].reshape(nb, SB_WIDTH)


def _da_step_kernel(pt_ref, lam_ref, q_ref, kn_ref, vn_ref, rbt_ref, bl_ref, *refs, n_steps, out_scale):
    del pt_ref
    k_refs = refs[:PAGES_PER_STEP]
    v_refs = refs[PAGES_PER_STEP:2 * PAGES_PER_STEP]
    g_ref = refs[2 * PAGES_PER_STEP]
    o_ref = refs[2 * PAGES_PER_STEP + 1]
    m_ref, l_ref, acc_ref, blast_ref = refs[2 * PAGES_PER_STEP + 2:]
    p = pl.program_id(1)
    half = ATTN_ROWS // 2
    page = v_refs[0].shape[1]
    rowid = lax.broadcasted_iota(jnp.int32, (ATTN_ROWS, HEAD_DIM), 0)
    vrow = lax.broadcasted_iota(jnp.int32, (ATTN_ROWS, DA_VDIM), 0) % half
    q = q_ref[...]
    q_rows = [jnp.where(rowid == half * (j % 2) + j // 2, q, 0.0).astype(BF16) for j in range(DA_MAPS)]
    bias_far = rbt_ref[:, REL_BUCKETS - 1:REL_BUCKETS]

    @pl.when(p == 0)
    def _():
        prod = q.astype(BF16).astype(F32) * kn_ref[...].astype(BF16).astype(F32)
        s_self = jnp.sum(prod, axis=-1, keepdims=True) * ATTN_SCALE + rbt_ref[:, 0:1]
        m_ref[...] = jnp.broadcast_to(s_self, m_ref.shape)
        l_ref[...] = jnp.ones(l_ref.shape, F32)
        acc_ref[...] = vn_ref[...].astype(BF16).astype(F32)
        bias = jnp.zeros(blast_ref.shape, F32)
        for b in range(REL_BUCKETS):
            bias = jnp.where(bl_ref[...] == b, rbt_ref[:, b:b + 1], bias)
        blast_ref[...] = bias

    m_run = m_ref[:, 0:1]
    l_run = l_ref[:, 0:1]
    acc = acc_ref[...]
    scs = []
    for s in range(PAGES_PER_STEP):
        bias = bias_far
        if s == 0:
            bias = jnp.where(p == 0, blast_ref[...], bias_far)
        parts = []
        for j0 in (0, DA_HEADS):
            part = None
            for j in range(j0, j0 + DA_HEADS):
                kj = k_refs[s][pl.ds(j, page, stride=DA_MAPS), :].astype(BF16)
                d = _dot_nt(q_rows[j], kj)
                part = d if part is None else part + d
            parts.append(part)
        scs.append((parts[0] + parts[1]) * ATTN_SCALE + bias)
    m_new = m_run
    for sc in scs:
        m_new = jnp.maximum(m_new, jnp.max(sc, axis=-1, keepdims=True))
    alpha = jnp.exp(m_run - m_new)
    l_run = alpha * l_run
    acc_parts = []
    for s in range(PAGES_PER_STEP):
        pr = jnp.exp(scs[s] - m_new)
        l_run = l_run + jnp.sum(pr, axis=-1, keepdims=True)
        pb = pr.astype(BF16)
        part = jnp.zeros((ATTN_ROWS, DA_VDIM), F32)
        for h in range(DA_HEADS):
            part = part + jnp.where(vrow == h, _dot(pb, v_refs[s][h].astype(BF16)), 0.0)
        acc_parts.append(part)
    acc = alpha * acc + sum(acc_parts)
    m_run = m_new
    m_ref[...] = jnp.broadcast_to(m_run, m_ref.shape)
    l_ref[...] = jnp.broadcast_to(l_run, l_ref.shape)
    acc_ref[...] = acc

    @pl.when(p == n_steps - 1)
    def _():
        out = acc / l_run
        cvec = out[0:half, :] - lam_ref[0] * out[half:ATTN_ROWS, :]
        y = cvec * lax.rsqrt(jnp.mean(cvec * cvec, axis=-1, keepdims=True) + EPS)
        o_ref[...] = y * g_ref[...] * out_scale


def da_step_attention(q, k_new, v_new, cache_k, cache_v, page_table, lam, rel_bias, subln, layer, out_scale):
    nb, n_pages = page_table.shape
    page = cache_v.shape[3]
    n_steps = n_pages // PAGES_PER_STEP
    half = ATTN_ROWS // 2

    def map_rows(a):
        a = jnp.swapaxes(a.reshape(nb, DA_HEADS, 2, HEAD_DIM), 1, 2)
        a = jnp.pad(a, ((0, 0), (0, 0), (0, half - DA_HEADS), (0, 0)))
        return a.reshape(nb, ATTN_ROWS, HEAD_DIM)

    v_rows = jnp.pad(v_new.reshape(nb, DA_HEADS, DA_VDIM), ((0, 0), (0, half - DA_HEADS), (0, 0)))
    v_rows = jnp.concatenate([v_rows, v_rows], axis=1)
    rbt = jnp.zeros((ATTN_ROWS, REL_BUCKETS), F32)
    rbt = rbt.at[0:DA_HEADS].set(rel_bias.T).at[half:half + DA_HEADS].set(rel_bias.T)
    bucket_last = rel_bucket_of_distance(page - jnp.arange(page, dtype=jnp.int32)).reshape(1, page)
    row = pl.BlockSpec((None, ATTN_ROWS, HEAD_DIM), lambda b, p, pt: (b, 0, 0))
    const = lambda shape: pl.BlockSpec(shape, lambda b, p, pt: (0,) * len(shape))
    out = pl.pallas_call(
        functools.partial(_da_step_kernel, n_steps=n_steps, out_scale=out_scale),
        grid_spec=pltpu.PrefetchScalarGridSpec(
            num_scalar_prefetch=1,
            grid=(nb, n_steps),
            in_specs=[pl.BlockSpec(memory_space=pltpu.SMEM), row, row,
                      pl.BlockSpec((None, ATTN_ROWS, DA_VDIM), lambda b, p, pt: (b, 0, 0)),
                      const((ATTN_ROWS, REL_BUCKETS)), const((1, page))]
                     + _page_specs(layer, n_pages, (page * DA_MAPS, HEAD_DIM))
                     + _page_specs(layer, n_pages, (DA_HEADS, page, DA_VDIM)) + [const((1, DA_VDIM))],
            out_specs=pl.BlockSpec((None, half, DA_VDIM), lambda b, p, pt: (b, 0, 0)),
            scratch_shapes=[pltpu.VMEM((ATTN_ROWS, HEAD_DIM), F32), pltpu.VMEM((ATTN_ROWS, HEAD_DIM), F32),
                            pltpu.VMEM((ATTN_ROWS, DA_VDIM), F32), pltpu.VMEM((ATTN_ROWS, page), F32)],
        ),
        out_shape=jax.ShapeDtypeStruct((nb, half, DA_VDIM), F32),
        compiler_params=_params("parallel", "arbitrary"),
    )(page_table, lam.reshape(1), map_rows(q), map_rows(k_new), v_rows, rbt, bucket_last,
      *([cache_k] * PAGES_PER_STEP), *([cache_v] * PAGES_PER_STEP), subln.reshape(1, DA_VDIM))
    return out[:, :DA_HEADS].reshape(nb, DA_WIDTH)


W_IN_SEGMENTS = (2 * GM_WIDTH, SB_WIDTH, SB_WIDTH, SB_WIDTH, DA_WIDTH, DA_WIDTH, DA_WIDTH)


def _lambda(wts, layer):
    lam_init = 0.8 - 0.6 * math.exp(-0.3 * layer)
    lam = (jnp.exp(jnp.sum(wts['da_lq1'][layer] * wts['da_lk1'][layer]))
           - jnp.exp(jnp.sum(wts['da_lq2'][layer] * wts['da_lk2'][layer])) + lam_init)
    return lam.astype(F32), lam_init


def _prompt_layer(x, mem_prompt_bf16, wts, layer, bsz, t, stacked):
    lam, lam_init = _lambda(wts, layer)
    depth = wts['w_in'].shape[0]
    h = rmsnorm(x, wts['norm_mix'][layer], BF16)
    cuts = [0]
    for width in W_IN_SEGMENTS:
        cuts.append(cuts[-1] + width)
    w_in = wts['w_in']
    a_in = matmul(h, w_in, layer, col0=cuts[0], ncols=W_IN_SEGMENTS[0])
    sb_q = matmul_heads(h, w_in, layer, col0=cuts[1], ncols=SB_WIDTH, slab=HEAD_DIM, bsz=bsz, t=t, out_dtype=BF16)
    heads_out = dict(bsz=bsz, t=t, depth=depth, out_layer=layer)
    sb_k = matmul_heads(h, w_in, layer, col0=cuts[2], ncols=SB_WIDTH, slab=HEAD_DIM, stacked=stacked[0], **heads_out)
    sb_v = matmul_heads(h, w_in, layer, col0=cuts[3], ncols=SB_WIDTH, slab=HEAD_DIM, stacked=stacked[1], **heads_out)
    da_q = matmul(h, w_in, layer, col0=cuts[4], ncols=DA_WIDTH, out_dtype=BF16)
    da_k, da_k_rows = matmul_token_rows(h, w_in, layer, col0=cuts[5], ncols=DA_WIDTH, stacked=stacked[3],
                                        **heads_out)
    da_v = matmul_heads(h, w_in, layer, col0=cuts[6], ncols=DA_WIDTH, slab=DA_VDIM, stacked=stacked[2], **heads_out)

    a_out, _ = gmlp(a_in, wts['gm_ln_g'][layer], wts['gm_ln_b'][layer], wts['gm_ws'][layer],
                    wts['gm_bs'][layer], wts['norm_a'][layer], single_token=False)
    b_out = rmsnorm(sb_attention(sb_q, sb_k, sb_v, layer, bsz, t), wts['norm_b'][layer], BF16)
    c_out = da_attention(da_q, da_k, da_v, layer, lam, wts['rel_bias'], wts['da_subln'][layer], bsz, t,
                         1.0 - lam_init)
    x = matmul_parts([a_out, b_out, c_out], wts['w_out'], layer, x)

    mem_len = mem_prompt_bf16.shape[0] // bsz
    mk = matmul(mem_prompt_bf16, wts['w_mk'], layer)
    mv = matmul(mem_prompt_bf16, wts['w_mv'], layer)
    x, hf = mem_block(x, mk, mv, wts['norm_mem'][layer], wts['w_mq'], wts['w_mo'], wts['norm_ffn'][layer],
                      layer, bsz, t, mem_len)

    act, pconv = ffn_gate_up(hf, wts['w_gate'], wts['w_up'], wts['conv_w'], wts['conv_b'], layer, bsz, t)
    x = matmul(act, wts['w_down'], layer, nk=2, res=x)
    return x, (sb_k, sb_v, da_v, da_k_rows), (mk, mv, pconv)


def _pad_rows(a, rows, cols=None):
    cols = a.shape[1] if cols is None else cols
    return jnp.pad(a, ((0, rows - a.shape[0]), (0, cols - a.shape[1])))


def _sample_layer(x, caches, mem_k, mem_v, conv_state, page_table, wts, layer, nb):
    lam, lam_init = _lambda(wts, layer)
    cache_sb_k, cache_sb_v, cache_da_k, cache_da_v = caches
    h = rmsnorm(x, wts['norm_mix'][layer], BF16)
    proj = matmul(h, wts['w_in'], layer, tn=1024)
    cuts = [0]
    for width in W_IN_SEGMENTS:
        cuts.append(cuts[-1] + width)
    a_in, sb_q, sb_k, sb_v, da_q, da_k, da_v = (proj[:, cuts[i]:cuts[i + 1]] for i in range(7))

    a_out, gv = gmlp(a_in, wts['gm_ln_g'][layer], wts['gm_ln_b'][layer], wts['gm_ws'][layer],
                     wts['gm_bs'][layer], wts['norm_a'][layer], single_token=True)
    b_out = sb_step_attention(sb_q[:nb], cache_sb_k, cache_sb_v, page_table, wts['norm_b'][layer], layer)
    c_out = da_step_attention(da_q[:nb], da_k[:nb], da_v[:nb], cache_da_k, cache_da_v, page_table, lam,
                              wts['rel_bias'], wts['da_subln'][layer], layer, 1.0 - lam_init)
    mix = jnp.concatenate([a_out, _pad_rows(b_out, SAMPLE_ROWS).astype(BF16),
                           _pad_rows(c_out, SAMPLE_ROWS).astype(BF16)], axis=-1)
    x = matmul(mix, wts['w_out'], layer, tn=1024, res=x)

    hm = rmsnorm(x, wts['norm_mem'][layer], BF16)
    mq = matmul(hm, wts['w_mq'], layer, out_dtype=BF16)
    mem_len = mem_k.shape[2]
    mq_rep = jnp.repeat(mq[:nb], SAMPLE_ROWS, axis=0)
    mo = mem_attention(mq_rep, mem_k[layer].reshape(nb * mem_len, MEM_WIDTH),
                       mem_v[layer].reshape(nb * mem_len, MEM_WIDTH), nb, SAMPLE_ROWS, mem_len)
    mo = _pad_rows(mo.reshape(nb, SAMPLE_ROWS, MEM_WIDTH)[:, 0], SAMPLE_ROWS)
    x = matmul(mo, wts['w_mo'], layer, tn=1024, res=x)

    hf = rmsnorm(x, wts['norm_ffn'][layer], BF16)
    prefix = conv_state[layer]
    act, g = ffn_gate_up_step(hf, wts['w_gate'], wts['w_up'], wts['conv_w'], wts['conv_b'],
                              _pad_rows(prefix[:, 0], SAMPLE_ROWS), _pad_rows(prefix[:, 1], SAMPLE_ROWS), layer)
    x = matmul(act, wts['w_down'], layer, tn=1024, nk=2, res=x)
    sconv = jnp.stack([prefix[:, 1], g[:nb]], axis=1)
    return x, (sb_k[:nb], sb_v[:nb], da_k[:nb], da_v[:nb], gv[:nb], sconv)


def kernel(x_prompt, x_sample, cache_sb_k, cache_sb_v, cache_da_k, cache_da_v, cache_mem_k, cache_mem_v,
           state_conv, page_table, mem_prompt, norm_mix, w_in, gm_ln_g, gm_ln_b, gm_ws, gm_bs, norm_a, norm_b,
           da_lq1, da_lk1, da_lq2, da_lk2, da_subln, rel_bias, w_out, norm_mem, w_mq, w_mk, w_mv, w_mo,
           norm_ffn, w_gate, conv_w, conv_b, w_up, w_down, norm_final):
    bsz, t, d = x_prompt.shape
    nb = x_sample.shape[0]
    depth = w_in.shape[0]
    n_pool, page = cache_sb_k.shape[1], cache_sb_k.shape[2]
    mem_len = mem_prompt.shape[1]
    d_ff = w_gate.shape[2]
    wts = {
        'norm_mix': norm_mix, 'w_in': w_in.astype(BF16), 'gm_ln_g': gm_ln_g, 'gm_ln_b': gm_ln_b, 'gm_ws': gm_ws,
        'gm_bs': gm_bs, 'norm_a': norm_a, 'norm_b': norm_b, 'da_lq1': da_lq1, 'da_lk1': da_lk1,
        'da_lq2': da_lq2, 'da_lk2': da_lk2, 'da_subln': da_subln, 'rel_bias': rel_bias,
        'w_out': w_out.astype(BF16), 'norm_mem': norm_mem, 'w_mq': w_mq.astype(BF16),
        'w_mk': w_mk.astype(BF16), 'w_mv': w_mv.astype(BF16), 'w_mo': w_mo.astype(BF16),
        'norm_ffn': norm_ffn, 'w_gate': w_gate.astype(BF16), 'conv_w': conv_w, 'conv_b': conv_b,
        'w_up': w_up.astype(BF16), 'w_down': w_down.astype(BF16),
    }
    caches = (jnp.swapaxes(cache_sb_k, 2, 3), jnp.swapaxes(cache_sb_v, 2, 3),
              cache_da_k.reshape(depth, n_pool, page * DA_MAPS, HEAD_DIM), jnp.swapaxes(cache_da_v, 2, 3))
    mem_k = cache_mem_k.reshape(depth, nb, cache_mem_k.shape[2], MEM_WIDTH)
    mem_v = cache_mem_v.reshape(depth, nb, cache_mem_v.shape[2], MEM_WIDTH)
    mem_prompt_bf16 = mem_prompt.reshape(bsz * mem_len, d).astype(BF16)

    xp = x_prompt.reshape(bsz * t, d)
    xs = _pad_rows(x_sample.reshape(nb, d), SAMPLE_ROWS)
    p_new, s_new = [], []
    stacked = (None, None, None, None)
    for layer in range(depth):
        xp, stacked, p_state = _prompt_layer(xp, mem_prompt_bf16, wts, layer, bsz, t, stacked)
        p_new.append(p_state)
        xs, s_state = _sample_layer(xs, caches, mem_k, mem_v, state_conv, page_table, wts, layer, nb)
        s_new.append(s_state)
    g_final = norm_final
    y_prompt = rmsnorm(xp, g_final, F32).reshape(bsz, t, d)
    y_sample = rmsnorm(xs, g_final, F32)[:nb].reshape(nb, 1, d)

    def stack(states, idx, shape):
        return jnp.stack([s[idx] for s in states]).reshape((depth,) + shape)

    return (
        y_prompt, y_sample,
        jnp.swapaxes(stacked[0], 2, 3), jnp.swapaxes(stacked[1], 2, 3),
        stacked[3].reshape(depth, bsz, t, DA_HEADS, 2, HEAD_DIM), jnp.swapaxes(stacked[2], 2, 3),
        stack(p_new, 0, (bsz, mem_len, MEM_HEADS, HEAD_DIM)), stack(p_new, 1, (bsz, mem_len, MEM_HEADS, HEAD_DIM)),
        stack(p_new, 2, (bsz, CONV_W - 1, d_ff)),
        stack(s_new, 0, (nb, 1, SB_HEADS, HEAD_DIM)), stack(s_new, 1, (nb, 1, SB_HEADS, HEAD_DIM)),
        stack(s_new, 2, (nb, 1, DA_HEADS, 2, HEAD_DIM)), stack(s_new, 3, (nb, 1, DA_HEADS, DA_VDIM)),
        stack(s_new, 4, (nb, 1, GM_GROUPS, HEAD_DIM)), stack(s_new, 5, (nb, CONV_W - 1, d_ff)),
    )
```

```python
import functools
import math

import jax
import jax.numpy as jnp
from jax import lax
from jax.experimental import pallas as pl
from jax.experimental.pallas import tpu as pltpu

F32 = jnp.float32
BF16 = jnp.bfloat16

HEAD_DIM = 128
GM_GROUPS = 8
GM_WIDTH = GM_GROUPS * HEAD_DIM
CHUNK = 128
SB_HEADS = 12
SB_WIDTH = SB_HEADS * HEAD_DIM
DA_HEADS = 6
DA_VDIM = 2 * HEAD_DIM
DA_WIDTH = DA_HEADS * DA_VDIM
MEM_HEADS = 4
MEM_WIDTH = MEM_HEADS * HEAD_DIM
REL_BUCKETS = 32
REL_MAX_DIST = 128
CONV_W = 3
ATTN_SCALE = HEAD_DIM ** -0.5
EPS = 1e-6
NEG = -1e30

SAMPLE_ROWS = 16
ATTN_ROWS = 16
VMEM_LIMIT_BYTES = 56 * 1024 * 1024
FFN_TILE = 512
FFN_SUB = 256
SB_DECAY_LIMIT = 110.0
STRIP = 32
SB_GROUP = 4
DA_GROUP = 2


def _params(*sem):
    return pltpu.CompilerParams(dimension_semantics=sem, vmem_limit_bytes=VMEM_LIMIT_BYTES)


def _dot(a, b):
    return jnp.dot(a, b, preferred_element_type=F32)


def _dot_nt(a, b):
    return lax.dot_general(a, b, (((1,), (1,)), ((), ())), preferred_element_type=F32)


def _softplus(z):
    return jnp.maximum(z, 0.0) + jnp.log(1.0 + jnp.exp(-jnp.abs(z)))


def _split_bf16(x):
    hi = x.astype(BF16)
    lo = (x - hi.astype(F32)).astype(BF16)
    return hi, lo


def _by_strips(fn, n_out, rows, *arrays):
    outs = [[] for _ in range(n_out)]
    for r0 in range(0, rows, STRIP):
        res = fn(r0, *[a[r0:r0 + STRIP] for a in arrays])
        for acc, r in zip(outs, res):
            acc.append(r)
    return [jnp.concatenate(o, axis=0) for o in outs]


def _rmsnorm_kernel(x_ref, g_ref, o_ref):
    x = x_ref[...].astype(F32)
    y = x * lax.rsqrt(jnp.mean(x * x, axis=-1, keepdims=True) + EPS)
    o_ref[...] = (y * g_ref[...]).astype(o_ref.dtype)


def rmsnorm(x, g, out_dtype, tm=256):
    m, d = x.shape
    tm = min(tm, m)
    return pl.pallas_call(
        _rmsnorm_kernel,
        grid=(m // tm,),
        in_specs=[pl.BlockSpec((tm, d), lambda i: (i, 0)),
                  pl.BlockSpec((1, d), lambda i: (0, 0))],
        out_specs=pl.BlockSpec((tm, d), lambda i: (i, 0)),
        out_shape=jax.ShapeDtypeStruct((m, d), out_dtype),
        compiler_params=_params("parallel"),
    )(x, g.reshape(1, d).astype(F32))


def _mm_kernel(*refs, nk, has_res):
    x_ref, w_ref = refs[0], refs[1]
    r_ref = refs[2] if has_res else None
    o_ref = refs[2 + has_res]
    part = _dot(x_ref[...], w_ref[...])
    if nk == 1:
        if has_res:
            part = part + r_ref[...]
        o_ref[...] = part.astype(o_ref.dtype)
        return
    acc_ref = refs[3 + has_res]
    k = pl.program_id(2)

    @pl.when(k == 0)
    def _():
        acc_ref[...] = part

    @pl.when(k > 0)
    def _():
        acc_ref[...] += part

    @pl.when(k == nk - 1)
    def _():
        r = acc_ref[...]
        if has_res:
            r = r + r_ref[...]
        o_ref[...] = r.astype(o_ref.dtype)


def matmul(x, w, layer, *, col0=0, ncols=None, tm=1024, tn=512, nk=1, res=None, out_dtype=F32):
    m, kdim = x.shape
    n_total = w.shape[2]
    ncols = n_total if ncols is None else ncols
    tm = min(tm, m)
    tn = min(tn, ncols)
    while ncols % tn or col0 % tn:
        tn //= 2
    tk = kdim // nk
    jb = col0 // tn
    in_specs = [pl.BlockSpec((tm, tk), lambda i, j, k: (i, k)),
                pl.BlockSpec((None, tk, tn), lambda i, j, k: (layer, k, jb + j))]
    args = [x, w]
    if res is not None:
        in_specs.append(pl.BlockSpec((tm, tn), lambda i, j, k: (i, j)))
        args.append(res)
    scratch = [pltpu.VMEM((tm, tn), F32)] if nk > 1 else []
    return pl.pallas_call(
        functools.partial(_mm_kernel, nk=nk, has_res=res is not None),
        grid=(m // tm, ncols // tn, nk),
        in_specs=in_specs,
        out_specs=pl.BlockSpec((tm, tn), lambda i, j, k: (i, j)),
        out_shape=jax.ShapeDtypeStruct((m, ncols), out_dtype),
        scratch_shapes=scratch,
        compiler_params=_params("parallel", "parallel", "arbitrary"),
    )(*args)


def _mm_parts_kernel(*refs, n_x):
    x_refs = refs[:n_x]
    w_ref, r_ref, o_ref = refs[n_x:]
    acc = r_ref[...]
    k0 = 0
    for x_ref in x_refs:
        k = x_ref.shape[1]
        acc = acc + _dot(x_ref[...], w_ref[k0:k0 + k, :])
        k0 += k
    o_ref[...] = acc.astype(o_ref.dtype)


def matmul_parts(xs, w, layer, res, *, tm=1024, tn=1024):
    m = xs[0].shape[0]
    kdim, n = w.shape[1], w.shape[2]
    tm = min(tm, m)
    tn = min(tn, n)
    in_specs = [pl.BlockSpec((tm, x.shape[1]), lambda i, j: (i, 0)) for x in xs]
    in_specs += [pl.BlockSpec((None, kdim, tn), lambda i, j: (layer, 0, j)),
                 pl.BlockSpec((tm, tn), lambda i, j: (i, j))]
    return pl.pallas_call(
        functools.partial(_mm_parts_kernel, n_x=len(xs)),
        grid=(m // tm, n // tn),
        in_specs=in_specs,
        out_specs=pl.BlockSpec((tm, tn), lambda i, j: (i, j)),
        out_shape=jax.ShapeDtypeStruct((m, n), F32),
        compiler_params=_params("parallel", "parallel"),
    )(*xs, w, res)


def _mm_heads_kernel(*refs, slab):
    x_ref, w_ref, o_ref = refs[0], refs[1], refs[-1]
    part = _dot(x_ref[...], w_ref[...])
    for s in range(o_ref.shape[0]):
        o_ref[s] = part[:, s * slab:(s + 1) * slab].astype(o_ref.dtype)


def matmul_heads(x, w, layer, *, col0, ncols, slab, bsz, t, depth=1, out_layer=0, stacked=None,
                 tm=1024, tn=512, out_dtype=F32):
    m, kdim = x.shape
    tm = min(tm, t)
    tn = min(tn, ncols)
    jb = col0 // tn
    tiles_per_seq = t // tm
    in_specs = [pl.BlockSpec((tm, kdim), lambda i, j: (i, 0)),
                pl.BlockSpec((None, kdim, tn), lambda i, j: (layer, 0, jb + j))]
    args = [x, w]
    aliases = {}
    if stacked is not None:
        in_specs.append(pl.BlockSpec(memory_space=pl.ANY))
        args.append(stacked)
        aliases = {2: 0}
    return pl.pallas_call(
        functools.partial(_mm_heads_kernel, slab=slab),
        grid=(m // tm, ncols // tn),
        in_specs=in_specs,
        out_specs=pl.BlockSpec((None, None, tn // slab, tm, slab),
                               lambda i, j: (out_layer, i // tiles_per_seq, j, i % tiles_per_seq, 0)),
        out_shape=jax.ShapeDtypeStruct((depth, bsz, ncols // slab, t, slab), out_dtype),
        input_output_aliases=aliases,
        compiler_params=_params("parallel", "parallel"),
    )(*args)


def _mm_token_rows_kernel(*refs, n_w):
    x_ref, w_refs = refs[0], refs[1:1 + n_w]
    std_ref, rows_ref = refs[-2], refs[-1]
    tm = x_ref.shape[0]
    tn = w_refs[0].shape[1]
    per_w = tn // HEAD_DIM
    n_blocks = n_w * per_w
    x = x_ref[...]
    for s in range(n_w):
        part = _dot(x, w_refs[s][...])
        std_ref[:, s * tn:(s + 1) * tn] = part.astype(std_ref.dtype)
        for c in range(per_w):
            rows_ref[pl.ds(s * per_w + c, tm, stride=n_blocks), :] = part[:, c * HEAD_DIM:(c + 1) * HEAD_DIM]


def matmul_token_rows(x, w, layer, *, col0, ncols, bsz, t, depth, out_layer, stacked, tm=512, tn=512):
    m, kdim = x.shape
    tm = min(tm, t)
    n_w = ncols // tn
    n_blocks = ncols // HEAD_DIM
    tiles_per_seq = t // tm
    in_specs = [pl.BlockSpec((tm, kdim), lambda i: (i, 0))]
    in_specs += [pl.BlockSpec((None, kdim, tn), lambda i, s=s: (layer, 0, col0 // tn + s)) for s in range(n_w)]
    args = [x] + [w] * n_w
    aliases = {}
    if stacked is not None:
        in_specs.append(pl.BlockSpec(memory_space=pl.ANY))
        args.append(stacked)
        aliases = {1 + n_w: 1}
    return pl.pallas_call(
        functools.partial(_mm_token_rows_kernel, n_w=n_w),
        grid=(m // tm,),
        in_specs=in_specs,
        out_specs=[pl.BlockSpec((tm, ncols), lambda i: (i, 0)),
                   pl.BlockSpec((None, None, tm * n_blocks, HEAD_DIM),
                                lambda i: (out_layer, i // tiles_per_seq, i % tiles_per_seq, 0))],
        out_shape=[jax.ShapeDtypeStruct((m, ncols), BF16),
                   jax.ShapeDtypeStruct((depth, bsz, t * n_blocks, HEAD_DIM), F32)],
        input_output_aliases=aliases,
        compiler_params=_params("parallel"),
    )(*args)


def _gmlp_kernel(a_ref, lng_ref, lnb_ref, ws_ref, bs_ref, na_ref, o_ref, gv_ref, *, single_token):
    uv = jax.nn.gelu(a_ref[...])
    if not single_token:
        r = lax.broadcasted_iota(jnp.int32, (CHUNK, CHUNK), 0)
        c = lax.broadcasted_iota(jnp.int32, (CHUNK, CHUNK), 1)
        tril = r >= c
    outs = []
    for g in range(GM_GROUPS):
        lo, hi = g * HEAD_DIM, (g + 1) * HEAD_DIM
        u = uv[:, lo:hi]
        v = uv[:, GM_WIDTH + lo:GM_WIDTH + hi]
        vc = v - jnp.mean(v, axis=-1, keepdims=True)
        y = vc * lax.rsqrt(jnp.mean(vc * vc, axis=-1, keepdims=True) + EPS)
        gv = y * lng_ref[:, lo:hi] + lnb_ref[:, lo:hi]
        gv_ref[:, lo:hi] = gv
        if single_token:
            mix = ws_ref[:, lo:hi] * gv + bs_ref[:, lo:hi]
        else:
            wm = jnp.where(tril, ws_ref[g], 0.0).astype(BF16)
            mix = _dot(wm, gv.astype(BF16)) + bs_ref[:, g:g + 1]
        outs.append(u * mix)
    t = jnp.concatenate(outs, axis=-1)
    y = t * lax.rsqrt(jnp.mean(t * t, axis=-1, keepdims=True) + EPS)
    o_ref[...] = (y * na_ref[...]).astype(o_ref.dtype)


def gmlp(a_in, ln_g, ln_b, ws, bs, norm_a, *, single_token):
    m = a_in.shape[0]
    rows = m if single_token else CHUNK
    full = lambda shape: pl.BlockSpec(shape, lambda i: (0,) * len(shape))
    if single_token:
        ws_arg = jnp.repeat(ws[:, 0, 0], HEAD_DIM).reshape(1, GM_WIDTH)
        bs_arg = jnp.repeat(bs[:, 0], HEAD_DIM).reshape(1, GM_WIDTH)
    else:
        ws_arg = ws
        bs_arg = bs.T
    return pl.pallas_call(
        functools.partial(_gmlp_kernel, single_token=single_token),
        grid=(m // rows,),
        in_specs=[pl.BlockSpec((rows, 2 * GM_WIDTH), lambda i: (i, 0)),
                  full((1, GM_WIDTH)), full((1, GM_WIDTH)),
                  full(ws_arg.shape), full(bs_arg.shape), full((1, GM_WIDTH))],
        out_specs=[pl.BlockSpec((rows, GM_WIDTH), lambda i: (i, 0)),
                   pl.BlockSpec((rows, GM_WIDTH), lambda i: (i, 0))],
        out_shape=[jax.ShapeDtypeStruct((m, GM_WIDTH), BF16),
                   jax.ShapeDtypeStruct((m, GM_WIDTH), F32)],
        compiler_params=_params("parallel"),
    )(a_in, ln_g.reshape(1, GM_WIDTH), ln_b.reshape(1, GM_WIDTH), ws_arg, bs_arg,
      norm_a.reshape(1, GM_WIDTH))


def _sb_kernel(q_ref, k_ref, v_ref, o_ref, kb_ref, vb_ref, *, t, tq):
    group = q_ref.shape[0]
    heads = range(group)
    kb_ref[...] = k_ref[...].astype(BF16)
    vb_ref[...] = v_ref[...].astype(BF16)
    r = lax.broadcasted_iota(jnp.int32, (tq, tq), 0)
    c = lax.broadcasted_iota(jnp.int32, (tq, tq), 1)
    later = jnp.where(r > c, 1.0, 0.0).astype(BF16)
    later2 = jnp.concatenate([later, later], axis=0)
    srow = lax.broadcasted_iota(jnp.int32, (STRIP, tq), 0)
    scol = lax.broadcasted_iota(jnp.int32, (STRIP, tq), 1)

    def tile(qb, kj, state, masked):
        start = pl.multiple_of(kj * tq, tq)
        z = [_dot_nt(qb[g], kb_ref[g, pl.ds(start, tq), :]) for g in heads]

        def front(r0, zs):
            t = zs * ATTN_SCALE
            p = _softplus(t)
            tmp = t - p
            if masked:
                p = jnp.where(scol < srow + r0, p, 0.0)
            hi, lo = _split_bf16(p)
            return jnp.concatenate([hi, lo], axis=1), tmp, p[:, 0:1]

        fr = [_by_strips(front, 3, tq, z[g]) for g in heads]
        cum = [_dot(fr[g][0], later2) for g in heads]

        def back(r0, tmp, cm, run):
            a = jnp.exp(tmp - (run + cm))
            if masked:
                a = jnp.where(scol < srow + r0, a, 0.0)
            return (a.astype(BF16),)

        a = [_by_strips(back, 1, tq, fr[g][1], cum[g], state[g][0])[0] for g in heads]
        pv = [_dot(a[g], vb_ref[g, pl.ds(start, tq), :]) for g in heads]
        return tuple((state[g][0] + cum[g][:, 0:1] + fr[g][2], state[g][1] + pv[g]) for g in heads)

    def q_body(qi, carry):
        qstart = pl.multiple_of(qi * tq, tq)
        qb = [q_ref[g, pl.ds(qstart, tq), :] for g in heads]
        init = tuple((jnp.zeros((tq, 1), F32), jnp.zeros((tq, HEAD_DIM), F32)) for _ in heads)
        state = tile(qb, qi, init, True)

        def cond(carry):
            j, st = carry
            low = st[0][0]
            for g in heads[1:]:
                low = jnp.minimum(low, st[g][0])
            return jnp.logical_and(j < qi, jnp.min(low) < SB_DECAY_LIMIT)

        _, state = lax.while_loop(cond, lambda cr: (cr[0] + 1, tile(qb, qi - 1 - cr[0], cr[1], False)),
                                  (jnp.int32(0), state))
        o_ref[pl.ds(qstart, tq), :] = jnp.concatenate([acc for _, acc in state], axis=1)
        return carry

    lax.fori_loop(0, t // tq, q_body, 0)


def _attn_tq(t):
    return 256 if t % 256 == 0 and t >= 1024 else 128


def sb_attention(q, k, v, layer, bsz, t):
    tq = _attn_tq(t)
    group = SB_GROUP

    def spec(lead):
        return pl.BlockSpec((None, None, group, t, HEAD_DIM), lambda b, h: (lead, b, h, 0, 0))

    return pl.pallas_call(
        functools.partial(_sb_kernel, t=t, tq=tq),
        grid=(bsz, SB_HEADS // group),
        in_specs=[spec(0), spec(layer), spec(layer)],
        out_specs=pl.BlockSpec((t, group * HEAD_DIM), lambda b, h: (b, h)),
        out_shape=jax.ShapeDtypeStruct((bsz * t, SB_WIDTH), F32),
        scratch_shapes=[pltpu.VMEM((group, t, HEAD_DIM), BF16), pltpu.VMEM((group, t, HEAD_DIM), BF16)],
        compiler_params=_params("parallel", "parallel"),
    )(q, k, v)


def rel_bucket_of_distance(n):
    max_exact = REL_BUCKETS // 2
    nf = jnp.maximum(n, 1).astype(F32)
    large = max_exact + (jnp.log(nf / max_exact) / math.log(REL_MAX_DIST / max_exact)
                         * (REL_BUCKETS - max_exact)).astype(jnp.int32)
    large = jnp.minimum(large, REL_BUCKETS - 1)
    return jnp.where(n < max_exact, n, large)


def _bias_from_buckets(buckets, rb_ref, head):
    bias = jnp.zeros(buckets.shape, F32)
    for b in range(REL_BUCKETS):
        bias = jnp.where(buckets == b, rb_ref[b, head], bias)
    return bias


def _da_kernel(lam_ref, rb_ref, q_ref, k_ref, v_ref, bd_ref, bo_ref, g_ref, o_ref,
               kb_ref, vb_ref, biasd_ref, biaso_ref, *, t, tq, out_scale):
    group = v_ref.shape[0]
    head0 = pl.program_id(1) * group
    lam = lam_ref[0]
    chains = [(g, mi) for g in range(group) for mi in range(2)]
    cols = [g * DA_VDIM + mi * HEAD_DIM for g, mi in chains]
    kb_ref[...] = k_ref[...].astype(BF16)
    vb_ref[...] = v_ref[...].astype(BF16)
    for g in range(group):
        biasd_ref[g] = _bias_from_buckets(bd_ref[...], rb_ref, head0 + g)
        biaso_ref[g] = _bias_from_buckets(bo_ref[...], rb_ref, head0 + g)
    bias_far = [rb_ref[REL_BUCKETS - 1, head0 + g] for g in range(group)]
    r = lax.broadcasted_iota(jnp.int32, (tq, tq), 0)
    c = lax.broadcasted_iota(jnp.int32, (tq, tq), 1)
    causal = c <= r

    def tile(qb, kj, state, bias, masked):
        start = pl.multiple_of(kj * tq, tq)
        kblk = kb_ref[pl.ds(start, tq), :]
        n = range(len(chains))
        s = [_dot_nt(qb[:, cols[i]:cols[i] + HEAD_DIM], kblk[:, cols[i]:cols[i] + HEAD_DIM]) * ATTN_SCALE
             + bias[chains[i][0]] for i in n]
        if masked:
            s = [jnp.where(causal, s[i], NEG) for i in n]
        m_new = [jnp.maximum(state[i][0], jnp.max(s[i], axis=-1, keepdims=True)) for i in n]
        alpha = [jnp.exp(state[i][0] - m_new[i]) for i in n]
        p = [jnp.exp(s[i] - m_new[i]) for i in n]
        l_new = [alpha[i] * state[i][1] + jnp.sum(p[i], axis=-1, keepdims=True) for i in n]
        pv = [_dot(p[i].astype(BF16), vb_ref[chains[i][0], pl.ds(start, tq), :]) for i in n]
        return tuple((m_new[i], l_new[i], alpha[i] * state[i][2] + pv[i]) for i in n)

    def q_body(qi, carry):
        qstart = pl.multiple_of(qi * tq, tq)
        qb = q_ref[pl.ds(qstart, tq), :]
        init = tuple((jnp.full((tq, 1), NEG, F32), jnp.zeros((tq, 1), F32), jnp.zeros((tq, DA_VDIM), F32))
                     for _ in chains)
        state = tile(qb, qi, init, [biasd_ref[g] for g in range(group)], True)

        def k_body(j, st):
            bias = [jnp.where(j == 0, biaso_ref[g], bias_far[g]) for g in range(group)]
            return tile(qb, qi - 1 - j, st, bias, False)

        state = lax.fori_loop(0, qi, k_body, state)
        outs = []
        for g in range(group):
            (_, l1, a1), (_, l2, a2) = state[2 * g], state[2 * g + 1]
            cvec = a1 / l1 - lam * (a2 / l2)
            y = cvec * lax.rsqrt(jnp.mean(cvec * cvec, axis=-1, keepdims=True) + EPS)
            outs.append(y * g_ref[...] * out_scale)
        o_ref[pl.ds(qstart, tq), :] = jnp.concatenate(outs, axis=1).astype(o_ref.dtype)
        return carry

    lax.fori_loop(0, t // tq, q_body, 0)


def da_attention(q, k, v, layer, lam, rel_bias, subln, bsz, t, out_scale):
    tq = _attn_tq(t)
    group = DA_GROUP
    pos = jnp.arange(tq, dtype=jnp.int32)
    dist = pos[:, None] - pos[None, :]
    bucket_diag = rel_bucket_of_distance(jnp.maximum(dist, 0))
    bucket_off = rel_bucket_of_distance(dist + tq)
    spec = pl.BlockSpec((t, group * DA_VDIM), lambda b, h: (b, h))
    smem = pl.BlockSpec(memory_space=pltpu.SMEM)
    const = lambda shape: pl.BlockSpec(shape, lambda b, h: (0,) * len(shape))
    return pl.pallas_call(
        functools.partial(_da_kernel, t=t, tq=tq, out_scale=out_scale),
        grid=(bsz, DA_HEADS // group),
        in_specs=[smem, smem, spec, spec,
                  pl.BlockSpec((None, None, group, t, DA_VDIM), lambda b, h: (layer, b, h, 0, 0)),
                  const((tq, tq)), const((tq, tq)), const((1, DA_VDIM))],
        out_specs=spec,
        out_shape=jax.ShapeDtypeStruct((bsz * t, DA_WIDTH), BF16),
        scratch_shapes=[pltpu.VMEM((t, group * DA_VDIM), BF16), pltpu.VMEM((group, t, DA_VDIM), BF16),
                        pltpu.VMEM((group, tq, tq), F32), pltpu.VMEM((group, tq, tq), F32)],
        compiler_params=_params("parallel", "parallel"),
    )(lam.reshape(1), rel_bias, q, k, v, bucket_diag, bucket_off, subln.reshape(1, DA_VDIM))


def _mem_kernel(q_ref, k_ref, v_ref, o_ref):
    outs = []
    for h in range(MEM_HEADS):
        lo, hi = h * HEAD_DIM, (h + 1) * HEAD_DIM
        s = _dot_nt(q_ref[:, lo:hi], k_ref[:, lo:hi].astype(BF16)) * ATTN_SCALE
        p = jnp.exp(s - jnp.max(s, axis=-1, keepdims=True))
        den = jnp.sum(p, axis=-1, keepdims=True)
        outs.append(_dot(p.astype(BF16), v_ref[:, lo:hi].astype(BF16)) / den)
    o_ref[...] = jnp.concatenate(outs, axis=-1).astype(o_ref.dtype)


def mem_attention(q, mk, mv, bsz, t, mem_len, tq=512):
    tq = min(tq, t)
    nq = t // tq
    kv_spec = pl.BlockSpec((mem_len, MEM_WIDTH), lambda b, i: (b, 0))
    return pl.pallas_call(
        _mem_kernel,
        grid=(bsz, nq),
        in_specs=[pl.BlockSpec((tq, MEM_WIDTH), lambda b, i: (b * nq + i, 0)), kv_spec, kv_spec],
        out_specs=pl.BlockSpec((tq, MEM_WIDTH), lambda b, i: (b * nq + i, 0)),
        out_shape=jax.ShapeDtypeStruct((bsz * t, MEM_WIDTH), BF16),
        compiler_params=_params("parallel", "parallel"),
    )(q, mk, mv)


def _mem_block_kernel(x_ref, gm_ref, wq_ref, k_ref, v_ref, wo_ref, gf_ref, xo_ref, hf_ref):
    x = x_ref[...]
    hm = (x * lax.rsqrt(jnp.mean(x * x, axis=-1, keepdims=True) + EPS) * gm_ref[...]).astype(BF16)
    mq = _dot(hm, wq_ref[...]).astype(BF16)
    outs = []
    for h in range(MEM_HEADS):
        lo, hi = h * HEAD_DIM, (h + 1) * HEAD_DIM
        s = _dot_nt(mq[:, lo:hi], k_ref[:, lo:hi].astype(BF16)) * ATTN_SCALE
        p = jnp.exp(s - jnp.max(s, axis=-1, keepdims=True))
        den = jnp.sum(p, axis=-1, keepdims=True)
        outs.append(_dot(p.astype(BF16), v_ref[:, lo:hi].astype(BF16)) / den)
    mo = jnp.concatenate(outs, axis=-1).astype(BF16)
    x_new = x + _dot(mo, wo_ref[...])
    xo_ref[...] = x_new
    y = x_new * lax.rsqrt(jnp.mean(x_new * x_new, axis=-1, keepdims=True) + EPS)
    hf_ref[...] = (y * gf_ref[...]).astype(hf_ref.dtype)


def mem_block(x, mk, mv, norm_mem, w_mq, w_mo, norm_ffn, layer, bsz, t, mem_len, tm=256):
    m, d = x.shape
    tm = min(tm, t)
    nq = t // tm
    rows = pl.BlockSpec((tm, d), lambda b, i: (b * nq + i, 0))
    gain = pl.BlockSpec((1, d), lambda b, i: (0, 0))
    kv = pl.BlockSpec((mem_len, MEM_WIDTH), lambda b, i: (b, 0))
    return pl.pallas_call(
        _mem_block_kernel,
        grid=(bsz, nq),
        in_specs=[rows, gain, pl.BlockSpec((None, d, MEM_WIDTH), lambda b, i: (layer, 0, 0)), kv, kv,
                  pl.BlockSpec((None, MEM_WIDTH, d), lambda b, i: (layer, 0, 0)), gain],
        out_specs=[rows, rows],
        out_shape=[jax.ShapeDtypeStruct((m, d), F32), jax.ShapeDtypeStruct((m, d), BF16)],
        compiler_params=_params("parallel", "parallel"),
    )(x, norm_mem.reshape(1, d), w_mq, mk, mv, w_mo, norm_ffn.reshape(1, d))


def _silu(x):
    return x / (1.0 + jnp.exp(-x))


def _ffn_kernel(h_ref, wg_ref, wu_ref, cw_ref, cb_ref, *rest, tm, tiles_per_seq):
    act_ref, pc_ref, gbuf_ref = rest[-3:]
    i = pl.program_id(1)
    first = (i % tiles_per_seq) == 0
    h = h_ref[...]
    for c in range(gbuf_ref.shape[0]):
        cols = slice(c * FFN_SUB, (c + 1) * FFN_SUB)
        buf = gbuf_ref.at[c]
        g = _dot(h, wg_ref[:, cols])
        u = _dot(h, wu_ref[:, cols])
        buf[0:8, :] = jnp.where(first, 0.0, buf[tm:tm + 8, :])
        buf[8:tm + 8, :] = g
        g1 = buf[7:tm + 7, :]
        g2 = buf[6:tm + 6, :]
        gc = cb_ref[:, cols] + cw_ref[0:1, cols] * g2 + cw_ref[1:2, cols] * g1 + cw_ref[2:3, cols] * g
        act_ref[:, cols] = (_silu(gc) * u).astype(act_ref.dtype)
        pc_ref[:, cols] = g[tm - (CONV_W - 1):tm, :]


def _ffn_gate_up_cols(h, wg, wu, conv_w, conv_b, layer, bsz, t, *, col0, ncols, tn, bufs, tm=1024):
    m, d = h.shape
    d_ff = wg.shape[2]
    tm = min(tm, t)
    jb = col0 // tn
    tiles_per_seq = t // tm
    wspec = pl.BlockSpec((None, d, tn), lambda j, i: (layer, 0, jb + j))
    in_specs = [pl.BlockSpec((tm, d), lambda j, i: (i, 0)), wspec, wspec,
                pl.BlockSpec((None, CONV_W, tn), lambda j, i: (layer, 0, jb + j)),
                pl.BlockSpec((None, 1, tn), lambda j, i: (layer, 0, jb + j))]
    args = [h, wg, wu, conv_w, conv_b.reshape(conv_b.shape[0], 1, d_ff)]
    aliases = {}
    if bufs is not None:
        in_specs += [pl.BlockSpec(memory_space=pl.ANY)] * 2
        args += list(bufs)
        aliases = {5: 0, 6: 1}
    return pl.pallas_call(
        functools.partial(_ffn_kernel, tm=tm, tiles_per_seq=tiles_per_seq),
        grid=(ncols // tn, m // tm),
        in_specs=in_specs,
        out_specs=[pl.BlockSpec((tm, tn), lambda j, i: (i, jb + j)),
                   pl.BlockSpec((None, CONV_W - 1, tn), lambda j, i: (i // tiles_per_seq, 0, jb + j))],
        out_shape=[jax.ShapeDtypeStruct((m, d_ff), BF16),
                   jax.ShapeDtypeStruct((bsz, CONV_W - 1, d_ff), F32)],
        scratch_shapes=[pltpu.VMEM((tn // FFN_SUB, tm + 8, FFN_SUB), F32)],
        input_output_aliases=aliases,
        compiler_params=_params("parallel", "arbitrary"),
    )(*args)


def ffn_gate_up(h, wg, wu, conv_w, conv_b, layer, bsz, t):
    d_ff = wg.shape[2]
    main = d_ff // FFN_TILE * FFN_TILE
    bufs = None
    if main:
        bufs = _ffn_gate_up_cols(h, wg, wu, conv_w, conv_b, layer, bsz, t, col0=0, ncols=main, tn=FFN_TILE,
                                 bufs=None)
    if d_ff > main:
        bufs = _ffn_gate_up_cols(h, wg, wu, conv_w, conv_b, layer, bsz, t, col0=main, ncols=d_ff - main,
                                 tn=d_ff - main, bufs=bufs)
    return bufs


def _ffn_step_kernel(h_ref, wg_ref, wu_ref, cw_ref, cb_ref, p0_ref, p1_ref, act_ref, g_ref):
    h = h_ref[...]
    g = _dot(h, wg_ref[...])
    u = _dot(h, wu_ref[...])
    gc = cb_ref[...] + cw_ref[0:1, :] * p0_ref[...] + cw_ref[1:2, :] * p1_ref[...] + cw_ref[2:3, :] * g
    act_ref[...] = (_silu(gc) * u).astype(act_ref.dtype)
    g_ref[...] = g


def ffn_gate_up_step(h, wg, wu, conv_w, conv_b, prefix0, prefix1, layer, *, tn=512):
    m, d = h.shape
    d_ff = wg.shape[2]
    while d_ff % tn:
        tn //= 2
    wspec = pl.BlockSpec((None, d, tn), lambda j: (layer, 0, j))
    row = pl.BlockSpec((m, tn), lambda j: (0, j))
    return pl.pallas_call(
        _ffn_step_kernel,
        grid=(d_ff // tn,),
        in_specs=[pl.BlockSpec((m, d), lambda j: (0, 0)), wspec, wspec,
                  pl.BlockSpec((None, CONV_W, tn), lambda j: (layer, 0, j)),
                  pl.BlockSpec((None, 1, tn), lambda j: (layer, 0, j)), row, row],
        out_specs=[row, row],
        out_shape=[jax.ShapeDtypeStruct((m, d_ff), BF16), jax.ShapeDtypeStruct((m, d_ff), F32)],
        compiler_params=_params("parallel"),
    )(h, wg, wu, conv_w, conv_b.reshape(conv_b.shape[0], 1, d_ff), prefix0, prefix1)


PAGES_PER_STEP = 8
DA_MAPS = 2 * DA_HEADS


def _page_specs(layer, n_pages, block):
    specs = []
    for s in range(PAGES_PER_STEP):
        def imap(b, p, pt, s=s):
            return (layer, pt[b, n_pages - 1 - (p * PAGES_PER_STEP + s)]) + (0,) * len(block)
        specs.append(pl.BlockSpec((None, None) + block, imap))
    return specs


def _sb_step_kernel(pt_ref, q_ref, kc_ref, vc_ref, g_ref, o_ref, kbuf_ref, vbuf_ref, sem_ref, *, layer, n_pages):
    b = pl.program_id(0)
    page = kbuf_ref.shape[2]
    rowid = lax.broadcasted_iota(jnp.int32, (ATTN_ROWS, HEAD_DIM), 0)
    q = q_ref[...]
    q_rows = [jnp.where(rowid == h, q, 0.0).astype(BF16) for h in range(SB_HEADS)]
    r = lax.broadcasted_iota(jnp.int32, (page, page), 0)
    c = lax.broadcasted_iota(jnp.int32, (page, page), 1)
    later = jnp.where(r > c, 1.0, 0.0).astype(BF16)
    later2 = jnp.concatenate([later, later], axis=0)
    half = SB_HEADS // 2

    def page_copies(i, slot):
        pg = pt_ref[b, n_pages - 1 - i]
        return (pltpu.make_async_copy(kc_ref.at[layer, pg], kbuf_ref.at[slot], sem_ref.at[0, slot]),
                pltpu.make_async_copy(vc_ref.at[layer, pg], vbuf_ref.at[slot], sem_ref.at[1, slot]))

    def not_decayed(run):
        return jnp.min(jnp.where(rowid[:, 0:1] < SB_HEADS, run, SB_DECAY_LIMIT)) < SB_DECAY_LIMIT

    def cond(carry):
        i, run, _ = carry
        return jnp.logical_and(i < n_pages, not_decayed(run))

    def body(carry):
        i, run, acc = carry
        slot = i % 2
        for cp in page_copies(i, slot):
            cp.wait()

        @pl.when(i + 1 < n_pages)
        def _():
            for cp in page_copies(i + 1, 1 - slot):
                cp.start()

        parts = []
        for h0 in (0, half):
            zp = _dot_nt(q_rows[h0], kbuf_ref[slot, h0].astype(BF16))
            for h in range(h0 + 1, h0 + half):
                zp = zp + _dot_nt(q_rows[h], kbuf_ref[slot, h].astype(BF16))
            parts.append(zp)
        t = (parts[0] + parts[1]) * ATTN_SCALE
        p = _softplus(t)
        hi, lo = _split_bf16(p)
        cum = _dot(jnp.concatenate([hi, lo], axis=1), later2)
        a = jnp.exp((t - p) - (run + cum)).astype(BF16)
        for h in range(SB_HEADS):
            acc = acc + jnp.where(rowid == h, _dot(a, vbuf_ref[slot, h].astype(BF16)), 0.0)
        return i + 1, run + jnp.sum(p, axis=-1, keepdims=True), acc

    for cp in page_copies(0, 0):
        cp.start()
    init = (jnp.int32(0), jnp.zeros((ATTN_ROWS, 1), F32), jnp.zeros((ATTN_ROWS, HEAD_DIM), F32))
    done, _, acc = lax.while_loop(cond, body, init)

    @pl.when(done < n_pages)
    def _():
        for cp in page_copies(done, done % 2):
            cp.wait()

    ms = jnp.sum(jnp.sum(acc * acc, axis=-1, keepdims=True), axis=0, keepdims=True) / SB_WIDTH
    o_ref[...] = acc * lax.rsqrt(ms + EPS) * g_ref[...]


def sb_step_attention(q, cache_k, cache_v, page_table, norm_b, layer):
    nb, n_pages = page_table.shape
    page = cache_k.shape[3]
    pad_heads = ((0, 0), (0, ATTN_ROWS - SB_HEADS), (0, 0))
    q_rows = jnp.pad(q.reshape(nb, SB_HEADS, HEAD_DIM), pad_heads)
    gain = jnp.pad(norm_b.reshape(SB_HEADS, HEAD_DIM), pad_heads[1:])
    row = pl.BlockSpec((None, ATTN_ROWS, HEAD_DIM), lambda b, pt: (b, 0, 0))
    hbm = pl.BlockSpec(memory_space=pl.ANY)
    out = pl.pallas_call(
        functools.partial(_sb_step_kernel, layer=layer, n_pages=n_pages),
        grid_spec=pltpu.PrefetchScalarGridSpec(
            num_scalar_prefetch=1,
            grid=(nb,),
            in_specs=[row, hbm, hbm, pl.BlockSpec((ATTN_ROWS, HEAD_DIM), lambda b, pt: (0, 0))],
            out_specs=row,
            scratch_shapes=[pltpu.VMEM((2, SB_HEADS, page, HEAD_DIM), F32),
                            pltpu.VMEM((2, SB_HEADS, page, HEAD_DIM), F32),
                            pltpu.SemaphoreType.DMA((2, 2))],
        ),
        out_shape=jax.ShapeDtypeStruct((nb, ATTN_ROWS, HEAD_DIM), F32),
        compiler_params=_params("arbitrary"),
    )(page_table, q_rows, cache_k, cache_v, gain)
    return out[:, :SB_HEADS].reshape(nb, SB_WIDTH)


def _da_step_kernel(pt_ref, lam_ref, q_ref, kn_ref, vn_ref, rbt_ref, bl_ref, *refs, n_steps, out_scale):
    del pt_ref
    k_refs = refs[:PAGES_PER_STEP]
    v_refs = refs[PAGES_PER_STEP:2 * PAGES_PER_STEP]
    g_ref = refs[2 * PAGES_PER_STEP]
    o_ref = refs[2 * PAGES_PER_STEP + 1]
    m_ref, l_ref, acc_ref, blast_ref = refs[2 * PAGES_PER_STEP + 2:]
    p = pl.program_id(1)
    half = ATTN_ROWS // 2
    page = v_refs[0].shape[1]
    rowid = lax.broadcasted_iota(jnp.int32, (ATTN_ROWS, HEAD_DIM), 0)
    vrow = lax.broadcasted_iota(jnp.int32, (ATTN_ROWS, DA_VDIM), 0) % half
    q = q_ref[...]
    q_rows = [jnp.where(rowid == half * (j % 2) + j // 2, q, 0.0).astype(BF16) for j in range(DA_MAPS)]
    bias_far = rbt_ref[:, REL_BUCKETS - 1:REL_BUCKETS]

    @pl.when(p == 0)
    def _():
        prod = q.astype(BF16).astype(F32) * kn_ref[...].astype(BF16).astype(F32)
        s_self = jnp.sum(prod, axis=-1, keepdims=True) * ATTN_SCALE + rbt_ref[:, 0:1]
        m_ref[...] = jnp.broadcast_to(s_self, m_ref.shape)
        l_ref[...] = jnp.ones(l_ref.shape, F32)
        acc_ref[...] = vn_ref[...].astype(BF16).astype(F32)
        bias = jnp.zeros(blast_ref.shape, F32)
        for b in range(REL_BUCKETS):
            bias = jnp.where(bl_ref[...] == b, rbt_ref[:, b:b + 1], bias)
        blast_ref[...] = bias

    m_run = m_ref[:, 0:1]
    l_run = l_ref[:, 0:1]
    acc = acc_ref[...]
    scs = []
    for s in range(PAGES_PER_STEP):
        bias = bias_far
        if s == 0:
            bias = jnp.where(p == 0, blast_ref[...], bias_far)
        parts = []
        for j0 in (0, DA_HEADS):
            part = None
            for j in range(j0, j0 + DA_HEADS):
                kj = k_refs[s][pl.ds(j, page, stride=DA_MAPS), :].astype(BF16)
                d = _dot_nt(q_rows[j], kj)
                part = d if part is None else part + d
            parts.append(part)
        scs.append((parts[0] + parts[1]) * ATTN_SCALE + bias)
    m_new = m_run
    for sc in scs:
        m_new = jnp.maximum(m_new, jnp.max(sc, axis=-1, keepdims=True))
    alpha = jnp.exp(m_run - m_new)
    l_run = alpha * l_run
    acc_parts = []
    for s in range(PAGES_PER_STEP):
        pr = jnp.exp(scs[s] - m_new)
        l_run = l_run + jnp.sum(pr, axis=-1, keepdims=True)
        pb = pr.astype(BF16)
        part = jnp.zeros((ATTN_ROWS, DA_VDIM), F32)
        for h in range(DA_HEADS):
            part = part + jnp.where(vrow == h, _dot(pb, v_refs[s][h].astype(BF16)), 0.0)
        acc_parts.append(part)
    acc = alpha * acc + sum(acc_parts)
    m_run = m_new
    m_ref[...] = jnp.broadcast_to(m_run, m_ref.shape)
    l_ref[...] = jnp.broadcast_to(l_run, l_ref.shape)
    acc_ref[...] = acc

    @pl.when(p == n_steps - 1)
    def _():
        out = acc / l_run
        cvec = out[0:half, :] - lam_ref[0] * out[half:ATTN_ROWS, :]
        y = cvec * lax.rsqrt(jnp.mean(cvec * cvec, axis=-1, keepdims=True) + EPS)
        o_ref[...] = y * g_ref[...] * out_scale


def da_step_attention(q, k_new, v_new, cache_k, cache_v, page_table, lam, rel_bias, subln, layer, out_scale):
    nb, n_pages = page_table.shape
    page = cache_v.shape[3]
    n_steps = n_pages // PAGES_PER_STEP
    half = ATTN_ROWS // 2

    def map_rows(a):
        a = jnp.swapaxes(a.reshape(nb, DA_HEADS, 2, HEAD_DIM), 1, 2)
        a = jnp.pad(a, ((0, 0), (0, 0), (0, half - DA_HEADS), (0, 0)))
        return a.reshape(nb, ATTN_ROWS, HEAD_DIM)

    v_rows = jnp.pad(v_new.reshape(nb, DA_HEADS, DA_VDIM), ((0, 0), (0, half - DA_HEADS), (0, 0)))
    v_rows = jnp.concatenate([v_rows, v_rows], axis=1)
    rbt = jnp.zeros((ATTN_ROWS, REL_BUCKETS), F32)
    rbt = rbt.at[0:DA_HEADS].set(rel_bias.T).at[half:half + DA_HEADS].set(rel_bias.T)
    bucket_last = rel_bucket_of_distance(page - jnp.arange(page, dtype=jnp.int32)).reshape(1, page)
    row = pl.BlockSpec((None, ATTN_ROWS, HEAD_DIM), lambda b, p, pt: (b, 0, 0))
    const = lambda shape: pl.BlockSpec(shape, lambda b, p, pt: (0,) * len(shape))
    out = pl.pallas_call(
        functools.partial(_da_step_kernel, n_steps=n_steps, out_scale=out_scale),
        grid_spec=pltpu.PrefetchScalarGridSpec(
            num_scalar_prefetch=1,
            grid=(nb, n_steps),
            in_specs=[pl.BlockSpec(memory_space=pltpu.SMEM), row, row,
                      pl.BlockSpec((None, ATTN_ROWS, DA_VDIM), lambda b, p, pt: (b, 0, 0)),
                      const((ATTN_ROWS, REL_BUCKETS)), const((1, page))]
                     + _page_specs(layer, n_pages, (page * DA_MAPS, HEAD_DIM))
                     + _page_specs(layer, n_pages, (DA_HEADS, page, DA_VDIM)) + [const((1, DA_VDIM))],
            out_specs=pl.BlockSpec((None, half, DA_VDIM), lambda b, p, pt: (b, 0, 0)),
            scratch_shapes=[pltpu.VMEM((ATTN_ROWS, HEAD_DIM), F32), pltpu.VMEM((ATTN_ROWS, HEAD_DIM), F32),
                            pltpu.VMEM((ATTN_ROWS, DA_VDIM), F32), pltpu.VMEM((ATTN_ROWS, page), F32)],
        ),
        out_shape=jax.ShapeDtypeStruct((nb, half, DA_VDIM), F32),
        compiler_params=_params("parallel", "arbitrary"),
    )(page_table, lam.reshape(1), map_rows(q), map_rows(k_new), v_rows, rbt, bucket_last,
      *([cache_k] * PAGES_PER_STEP), *([cache_v] * PAGES_PER_STEP), subln.reshape(1, DA_VDIM))
    return out[:, :DA_HEADS].reshape(nb, DA_WIDTH)


W_IN_SEGMENTS = (2 * GM_WIDTH, SB_WIDTH, SB_WIDTH, SB_WIDTH, DA_WIDTH, DA_WIDTH, DA_WIDTH)


def _lambda(wts, layer):
    lam_init = 0.8 - 0.6 * math.exp(-0.3 * layer)
    lam = (jnp.exp(jnp.sum(wts['da_lq1'][layer] * wts['da_lk1'][layer]))
           - jnp.exp(jnp.sum(wts['da_lq2'][layer] * wts['da_lk2'][layer])) + lam_init)
    return lam.astype(F32), lam_init


def _prompt_layer(x, mem_prompt_bf16, wts, layer, bsz, t, stacked):
    lam, lam_init = _lambda(wts, layer)
    depth = wts['w_in'].shape[0]
    h = rmsnorm(x, wts['norm_mix'][layer], BF16)
    cuts = [0]
    for width in W_IN_SEGMENTS:
        cuts.append(cuts[-1] + width)
    w_in = wts['w_in']
    a_in = matmul(h, w_in, layer, col0=cuts[0], ncols=W_IN_SEGMENTS[0])
    sb_q = matmul_heads(h, w_in, layer, col0=cuts[1], ncols=SB_WIDTH, slab=HEAD_DIM, bsz=bsz, t=t, out_dtype=BF16)
    heads_out = dict(bsz=bsz, t=t, depth=depth, out_layer=layer)
    sb_k = matmul_heads(h, w_in, layer, col0=cuts[2], ncols=SB_WIDTH, slab=HEAD_DIM, stacked=stacked[0], **heads_out)
    sb_v = matmul_heads(h, w_in, layer, col0=cuts[3], ncols=SB_WIDTH, slab=HEAD_DIM, stacked=stacked[1], **heads_out)
    da_q = matmul(h, w_in, layer, col0=cuts[4], ncols=DA_WIDTH, out_dtype=BF16)
    da_k, da_k_rows = matmul_token_rows(h, w_in, layer, col0=cuts[5], ncols=DA_WIDTH, stacked=stacked[3],
                                        **heads_out)
    da_v = matmul_heads(h, w_in, layer, col0=cuts[6], ncols=DA_WIDTH, slab=DA_VDIM, stacked=stacked[2], **heads_out)

    a_out, _ = gmlp(a_in, wts['gm_ln_g'][layer], wts['gm_ln_b'][layer], wts['gm_ws'][layer],
                    wts['gm_bs'][layer], wts['norm_a'][layer], single_token=False)
    b_out = rmsnorm(sb_attention(sb_q, sb_k, sb_v, layer, bsz, t), wts['norm_b'][layer], BF16)
    c_out = da_attention(da_q, da_k, da_v, layer, lam, wts['rel_bias'], wts['da_subln'][layer], bsz, t,
                         1.0 - lam_init)
    x = matmul_parts([a_out, b_out, c_out], wts['w_out'], layer, x)

    mem_len = mem_prompt_bf16.shape[0] // bsz
    mk = matmul(mem_prompt_bf16, wts['w_mk'], layer)
    mv = matmul(mem_prompt_bf16, wts['w_mv'], layer)
    x, hf = mem_block(x, mk, mv, wts['norm_mem'][layer], wts['w_mq'], wts['w_mo'], wts['norm_ffn'][layer],
                      layer, bsz, t, mem_len)

    act, pconv = ffn_gate_up(hf, wts['w_gate'], wts['w_up'], wts['conv_w'], wts['conv_b'], layer, bsz, t)
    x = matmul(act, wts['w_down'], layer, nk=2, res=x)
    return x, (sb_k, sb_v, da_v, da_k_rows), (mk, mv, pconv)


def _pad_rows(a, rows, cols=None):
    cols = a.shape[1] if cols is None else cols
    return jnp.pad(a, ((0, rows - a.shape[0]), (0, cols - a.shape[1])))


def _sample_layer(x, caches, mem_k, mem_v, conv_state, page_table, wts, layer, nb):
    lam, lam_init = _lambda(wts, layer)
    cache_sb_k, cache_sb_v, cache_da_k, cache_da_v = caches
    h = rmsnorm(x, wts['norm_mix'][layer], BF16)
    proj = matmul(h, wts['w_in'], layer, tn=1024)
    cuts = [0]
    for width in W_IN_SEGMENTS:
        cuts.append(cuts[-1] + width)
    a_in, sb_q, sb_k, sb_v, da_q, da_k, da_v = (proj[:, cuts[i]:cuts[i + 1]] for i in range(7))

    a_out, gv = gmlp(a_in, wts['gm_ln_g'][layer], wts['gm_ln_b'][layer], wts['gm_ws'][layer],
                     wts['gm_bs'][layer], wts['norm_a'][layer], single_token=True)
    b_out = sb_step_attention(sb_q[:nb], cache_sb_k, cache_sb_v, page_table, wts['norm_b'][layer], layer)
    c_out = da_step_attention(da_q[:nb], da_k[:nb], da_v[:nb], cache_da_k, cache_da_v, page_table, lam,
                              wts['rel_bias'], wts['da_subln'][layer], layer, 1.0 - lam_init)
    mix = jnp.concatenate([a_out, _pad_rows(b_out, SAMPLE_ROWS).astype(BF16),
                           _pad_rows(c_out, SAMPLE_ROWS).astype(BF16)], axis=-1)
    x = matmul(mix, wts['w_out'], layer, tn=1024, res=x)

    hm = rmsnorm(x, wts['norm_mem'][layer], BF16)
    mq = matmul(hm, wts['w_mq'], layer, out_dtype=BF16)
    mem_len = mem_k.shape[2]
    mq_rep = jnp.repeat(mq[:nb], SAMPLE_ROWS, axis=0)
    mo = mem_attention(mq_rep, mem_k[layer].reshape(nb * mem_len, MEM_WIDTH),
                       mem_v[layer].reshape(nb * mem_len, MEM_WIDTH), nb, SAMPLE_ROWS, mem_len)
    mo = _pad_rows(mo.reshape(nb, SAMPLE_ROWS, MEM_WIDTH)[:, 0], SAMPLE_ROWS)
    x = matmul(mo, wts['w_mo'], layer, tn=1024, res=x)

    hf = rmsnorm(x, wts['norm_ffn'][layer], BF16)
    prefix = conv_state[layer]
    act, g = ffn_gate_up_step(hf, wts['w_gate'], wts['w_up'], wts['conv_w'], wts['conv_b'],
                              _pad_rows(prefix[:, 0], SAMPLE_ROWS), _pad_rows(prefix[:, 1], SAMPLE_ROWS), layer)
    x = matmul(act, wts['w_down'], layer, tn=1024, nk=2, res=x)
    sconv = jnp.stack([prefix[:, 1], g[:nb]], axis=1)
    return x, (sb_k[:nb], sb_v[:nb], da_k[:nb], da_v[:nb], gv[:nb], sconv)


def kernel(x_prompt, x_sample, cache_sb_k, cache_sb_v, cache_da_k, cache_da_v, cache_mem_k, cache_mem_v,
           state_conv, page_table, mem_prompt, norm_mix, w_in, gm_ln_g, gm_ln_b, gm_ws, gm_bs, norm_a, norm_b,
           da_lq1, da_lk1, da_lq2, da_lk2, da_subln, rel_bias, w_out, norm_mem, w_mq, w_mk, w_mv, w_mo,
           norm_ffn, w_gate, conv_w, conv_b, w_up, w_down, norm_final):
    bsz, t, d = x_prompt.shape
    nb = x_sample.shape[0]
    depth = w_in.shape[0]
    n_pool, page = cache_sb_k.shape[1], cache_sb_k.shape[2]
    mem_len = mem_prompt.shape[1]
    d_ff = w_gate.shape[2]
    wts = {
        'norm_mix': norm_mix, 'w_in': w_in.astype(BF16), 'gm_ln_g': gm_ln_g, 'gm_ln_b': gm_ln_b, 'gm_ws': gm_ws,
        'gm_bs': gm_bs, 'norm_a': norm_a, 'norm_b': norm_b, 'da_lq1': da_lq1, 'da_lk1': da_lk1,
        'da_lq2': da_lq2, 'da_lk2': da_lk2, 'da_subln': da_subln, 'rel_bias': rel_bias,
        'w_out': w_out.astype(BF16), 'norm_mem': norm_mem, 'w_mq': w_mq.astype(BF16),
        'w_mk': w_mk.astype(BF16), 'w_mv': w_mv.astype(BF16), 'w_mo': w_mo.astype(BF16),
        'norm_ffn': norm_ffn, 'w_gate': w_gate.astype(BF16), 'conv_w': conv_w, 'conv_b': conv_b,
        'w_up': w_up.astype(BF16), 'w_down': w_down.astype(BF16),
    }
    caches = (jnp.swapaxes(cache_sb_k, 2, 3), jnp.swapaxes(cache_sb_v, 2, 3),
              cache_da_k.reshape(depth, n_pool, page * DA_MAPS, HEAD_DIM), jnp.swapaxes(cache_da_v, 2, 3))
    mem_k = cache_mem_k.reshape(depth, nb, cache_mem_k.shape[2], MEM_WIDTH)
    mem_v = cache_mem_v.reshape(depth, nb, cache_mem_v.shape[2], MEM_WIDTH)
    mem_prompt_bf16 = mem_prompt.reshape(bsz * mem_len, d).astype(BF16)

    xp = x_prompt.reshape(bsz * t, d)
    xs = _pad_rows(x_sample.reshape(nb, d), SAMPLE_ROWS)
    p_new, s_new = [], []
    stacked = (None, None, None, None)
    for layer in range(depth):
        xp, stacked, p_state = _prompt_layer(xp, mem_prompt_bf16, wts, layer, bsz, t, stacked)
        p_new.append(p_state)
        xs, s_state = _sample_layer(xs, caches, mem_k, mem_v, state_conv, page_table, wts, layer, nb)
        s_new.append(s_state)
    g_final = norm_final
    y_prompt = rmsnorm(xp, g_final, F32).reshape(bsz, t, d)
    y_sample = rmsnorm(xs, g_final, F32)[:nb].reshape(nb, 1, d)

    def stack(states, idx, shape):
        return jnp.stack([s[idx] for s in states]).reshape((depth,) + shape)

    return (
        y_prompt, y_sample,
        jnp.swapaxes(stacked[0], 2, 3), jnp.swapaxes(stacked[1], 2, 3),
        stacked[3].reshape(depth, bsz, t, DA_HEADS, 2, HEAD_DIM), jnp.swapaxes(stacked[2], 2, 3),
        stack(p_new, 0, (bsz, mem_len, MEM_HEADS, HEAD_DIM)), stack(p_new, 1, (bsz, mem_len, MEM_HEADS, HEAD_DIM)),
        stack(p_new, 2, (bsz, CONV_W - 1, d_ff)),
        stack(s_new, 0, (nb, 1, SB_HEADS, HEAD_DIM)), stack(s_new, 1, (nb, 1, SB_HEADS, HEAD_DIM)),
        stack(s_new, 2, (nb, 1, DA_HEADS, 2, HEAD_DIM)), stack(s_new, 3, (nb, 1, DA_HEADS, DA_VDIM)),
        stack(s_new, 4, (nb, 1, GM_GROUPS, HEAD_DIM)), stack(s_new, 5, (nb, CONV_W - 1, d_ff)),
    )
```
